```python
import jax, jax.numpy as jnp
from jax import lax
import numpy as np

D_MODEL = 1024
BATCH = 2
SEQ = 8192
DEPTH = 4
DEC_BATCH = 16
DEC_SEQ = 2048
PAST_LEN = 128

N_MIXERS = 4
D_FF = 4 * D_MODEL
EPS = 1e-6
POOL_WINDOWS = (2, 4, 8, 16)
N_POOL_GROUPS = len(POOL_WINDOWS)
POOL_GROUP = D_MODEL // N_POOL_GROUPS
CONV_WIDTH = 31
SGU_WIDTH = D_MODEL
SGU_HEADS = 4
SGU_HEAD_DIM = SGU_WIDTH // SGU_HEADS
SGU_CHUNK = 128
FNET_GROUPS = 4
FNET_GROUP = D_MODEL // FNET_GROUPS
N_POOL_LAYERS = (DEPTH - 0 + N_MIXERS - 1) // N_MIXERS
N_CONV_LAYERS = (DEPTH - 1 + N_MIXERS - 1) // N_MIXERS
N_SGU_LAYERS = (DEPTH - 2 + N_MIXERS - 1) // N_MIXERS
N_FNET_LAYERS = (DEPTH - 3 + N_MIXERS - 1) // N_MIXERS

kernel_name = "interleaved_pool_conv_sgu_fourier_encoder"


def rms_norm(x, g):
    xf = x.astype(jnp.float32)
    y = xf * lax.rsqrt(jnp.mean(xf * xf, axis=-1, keepdims=True) + EPS)
    return (y * g.astype(jnp.float32)).astype(x.dtype)


def layer_norm(x, g, b):
    xf = x.astype(jnp.float32)
    mu = jnp.mean(xf, axis=-1, keepdims=True)
    xc = xf - mu
    var = jnp.mean(xc * xc, axis=-1, keepdims=True)
    y = xc * lax.rsqrt(var + EPS) * g.astype(jnp.float32) + b.astype(jnp.float32)
    return y.astype(x.dtype)


def pool_mixer(h, w, scale):
    B, S, D = h.shape
    hf = h.astype(jnp.float32)
    cs = jnp.concatenate([jnp.zeros((B, 1, D), jnp.float32), jnp.cumsum(hf, axis=1)], axis=1)
    t = jnp.arange(S)
    outs = []
    for g, win in enumerate(POOL_WINDOWS):
        lo = jnp.maximum(t - win // 2, 0)
        hi = jnp.minimum(t + win // 2, S)
        sl = slice(g * POOL_GROUP, (g + 1) * POOL_GROUP)
        csg = cs[:, :, sl]
        cnt = (hi - lo).astype(jnp.float32)[None, :, None]
        mean = (jnp.take(csg, hi, axis=1) - jnp.take(csg, lo, axis=1)) / cnt
        outs.append(mean - hf[:, :, sl])
    d = jnp.stack(outs, axis=2).astype(h.dtype)
    y = jnp.einsum('bsgc,gce->bsge', d, w).reshape(B, S, D)
    return y * scale


def conv_mixer(h, w_in, b_in, dw, dw_b, ln_g, ln_b, w_out, b_out):
    a = jnp.einsum('bsd,de->bse', h, w_in) + b_in
    val, gate = jnp.split(a, 2, axis=-1)
    g = val * jax.nn.sigmoid(gate)
    pad = CONV_WIDTH // 2
    c = lax.conv_general_dilated(
        g, dw[:, None, :], window_strides=(1,), padding=[(pad, pad)],
        dimension_numbers=('NWC', 'WIO', 'NWC'), feature_group_count=D_MODEL) + dw_b
    c = jax.nn.silu(layer_norm(c, ln_g, ln_b))
    return jnp.einsum('bsd,de->bse', c, w_out) + b_out


def sgu_mixer(h, w_in, b_in, ln_g, ln_b, ws, bs, w_out, b_out):
    B, S, _ = h.shape
    z = jax.nn.gelu(jnp.einsum('bsd,de->bse', h, w_in) + b_in)
    u, v = jnp.split(z, 2, axis=-1)
    v = layer_norm(v, ln_g, ln_b)
    v = v.reshape(B, S // SGU_CHUNK, SGU_CHUNK, SGU_HEADS, SGU_HEAD_DIM)
    v = jnp.einsum('hqp,bcphd->bcqhd', ws, v) + bs.T[:, :, None]
    y = u * v.reshape(B, S, SGU_WIDTH)
    return jnp.einsum('bse,ed->bsd', y, w_out) + b_out


def fourier_mixer(h, w_out, b_out):
    B, S, D = h.shape
    hg = h.astype(jnp.float32).reshape(B, S, FNET_GROUPS, FNET_GROUP)
    f = jnp.fft.fftn(hg, axes=(1, 3), norm='ortho').real.astype(h.dtype).reshape(B, S, D)
    return jnp.einsum('bsd,de->bse', f, w_out) + b_out


def sqrelu_mlp(h, w_up, w_down):
    a = jax.nn.relu(jnp.einsum('bsd,df->bsf', h, w_up))
    return jnp.einsum('bsf,fd->bsd', a * a, w_down)


def trunk(x, norm_mix_g, norm_mlp_g, w_up, w_down, final_norm_g,
          pool_w, pool_scale,
          conv_w_in, conv_b_in, conv_dw, conv_dw_b, conv_ln_g, conv_ln_b, conv_w_out, conv_b_out,
          sgu_w_in, sgu_b_in, sgu_ln_g, sgu_ln_b, sgu_ws, sgu_bs, sgu_w_out, sgu_b_out,
          fnet_w_out, fnet_b_out):
    for i in range(DEPTH):
        m, j = i % N_MIXERS, i // N_MIXERS
        h = rms_norm(x, norm_mix_g[i])
        if m == 0:
            y = pool_mixer(h, pool_w[j], pool_scale[j])
        elif m == 1:
            y = conv_mixer(h, conv_w_in[j], conv_b_in[j], conv_dw[j], conv_dw_b[j],
                           conv_ln_g[j], conv_ln_b[j], conv_w_out[j], conv_b_out[j])
        elif m == 2:
            y = sgu_mixer(h, sgu_w_in[j], sgu_b_in[j], sgu_ln_g[j], sgu_ln_b[j],
                          sgu_ws[j], sgu_bs[j], sgu_w_out[j], sgu_b_out[j])
        else:
            y = fourier_mixer(h, fnet_w_out[j], fnet_b_out[j])
        x = x + y
        x = x + sqrelu_mlp(rms_norm(x, norm_mlp_g[i]), w_up[i], w_down[i])
    return rms_norm(x, final_norm_g)


def setup_inputs(seed: int = 0) -> dict:
    key = jax.random.key(seed)
    ks = jax.random.split(key, 32)
    D = D_MODEL

    def nrm(k, shape, scale):
        return jax.random.normal(k, shape, jnp.float32) * scale

    def gain(k, shape):
        return 1.0 + 0.05 * jax.random.normal(k, shape, jnp.float32)

    return {
        "x_prompt": nrm(ks[0], (BATCH, SEQ, D), 1.0),
        "x_sample": nrm(ks[1], (DEC_BATCH, DEC_SEQ, D), 1.0),
        "norm_mix_g": gain(ks[2], (DEPTH, D)),
        "norm_mlp_g": gain(ks[3], (DEPTH, D)),
        "w_up": nrm(ks[4], (DEPTH, D, D_FF), D ** -0.5),
        "w_down": nrm(ks[5], (DEPTH, D_FF, D), D_FF ** -0.5),
        "final_norm_g": gain(ks[6], (D,)),
        "pool_w": nrm(ks[7], (N_POOL_LAYERS, N_POOL_GROUPS, POOL_GROUP, POOL_GROUP), POOL_GROUP ** -0.5),
        "pool_scale": gain(ks[8], (N_POOL_LAYERS, D)),
        "conv_w_in": nrm(ks[9], (N_CONV_LAYERS, D, 2 * D), D ** -0.5),
        "conv_b_in": nrm(ks[10], (N_CONV_LAYERS, 2 * D), 0.02),
        "conv_dw": nrm(ks[11], (N_CONV_LAYERS, CONV_WIDTH, D), CONV_WIDTH ** -0.5),
        "conv_dw_b": nrm(ks[12], (N_CONV_LAYERS, D), 0.02),
        "conv_ln_g": gain(ks[13], (N_CONV_LAYERS, D)),
        "conv_ln_b": nrm(ks[14], (N_CONV_LAYERS, D), 0.02),
        "conv_w_out": nrm(ks[15], (N_CONV_LAYERS, D, D), D ** -0.5),
        "conv_b_out": nrm(ks[16], (N_CONV_LAYERS, D), 0.02),
        "sgu_w_in": nrm(ks[17], (N_SGU_LAYERS, D, 2 * SGU_WIDTH), D ** -0.5),
        "sgu_b_in": nrm(ks[18], (N_SGU_LAYERS, 2 * SGU_WIDTH), 0.02),
        "sgu_ln_g": gain(ks[19], (N_SGU_LAYERS, SGU_WIDTH)),
        "sgu_ln_b": nrm(ks[20], (N_SGU_LAYERS, SGU_WIDTH), 0.02),
        "sgu_ws": nrm(ks[21], (N_SGU_LAYERS, SGU_HEADS, SGU_CHUNK, SGU_CHUNK), SGU_CHUNK ** -0.5),
        "sgu_bs": gain(ks[22], (N_SGU_LAYERS, SGU_HEADS, SGU_CHUNK)),
        "sgu_w_out": nrm(ks[23], (N_SGU_LAYERS, SGU_WIDTH, D), SGU_WIDTH ** -0.5),
        "sgu_b_out": nrm(ks[24], (N_SGU_LAYERS, D), 0.02),
        "fnet_w_out": nrm(ks[25], (N_FNET_LAYERS, D, D), D ** -0.5),
        "fnet_b_out": nrm(ks[26], (N_FNET_LAYERS, D), 0.02),
    }


def reference(x_prompt, x_sample, norm_mix_g, norm_mlp_g, w_up, w_down, final_norm_g,
              pool_w, pool_scale,
              conv_w_in, conv_b_in, conv_dw, conv_dw_b, conv_ln_g, conv_ln_b, conv_w_out, conv_b_out,
              sgu_w_in, sgu_b_in, sgu_ln_g, sgu_ln_b, sgu_ws, sgu_bs, sgu_w_out, sgu_b_out,
              fnet_w_out, fnet_b_out):
    params = (norm_mix_g, norm_mlp_g, w_up, w_down, final_norm_g,
              pool_w, pool_scale,
              conv_w_in, conv_b_in, conv_dw, conv_dw_b, conv_ln_g, conv_ln_b, conv_w_out, conv_b_out,
              sgu_w_in, sgu_b_in, sgu_ln_g, sgu_ln_b, sgu_ws, sgu_bs, sgu_w_out, sgu_b_out,
              fnet_w_out, fnet_b_out)
    y_prompt = trunk(x_prompt, *params)
    y_sample = trunk(x_sample, *params)
    return (y_prompt, y_sample)
```

```python
import functools

import numpy as np
import jax
import jax.numpy as jnp
from jax import lax
from jax.experimental import pallas as pl
from jax.experimental.pallas import tpu as pltpu

D = 1024
D_FF = 4 * D
EPS = 1e-6
POOL_WINDOWS = (2, 4, 8, 16)
POOL_GROUP = D // len(POOL_WINDOWS)
POOL_HALO = 8
CONV_WIDTH = 31
CONV_HALO = 16
SGU_HEADS = 4
SGU_HEAD_DIM = D // SGU_HEADS
SGU_CHUNK = 128
FNET_GROUP = 256
FNET_BD = 256
FNET_R = 16

FF_CHUNK = 1024
ROW_TILE = 512
V7X_VMEM_LIMIT = 56 * 1024 * 1024

BF = jnp.bfloat16
F32 = jnp.float32


def _rms(x, g):
    return x * lax.rsqrt(jnp.mean(x * x, axis=-1, keepdims=True) + EPS) * g


def _layer_norm(x, g, b):
    mu = jnp.mean(x, axis=-1, keepdims=True)
    xc = x - mu
    var = jnp.mean(xc * xc, axis=-1, keepdims=True)
    return xc * lax.rsqrt(var + EPS) * g + b


def _dot(a, b):
    return jnp.dot(a, b, preferred_element_type=F32)


def _mlp_tail(x1, gmlp_ref, wup_ref, wdn_ref, gfin_ref):
    h = _rms(x1, gmlp_ref[...]).astype(BF)
    acc = x1
    for c in range(D_FF // FF_CHUNK):
        cs = slice(c * FF_CHUNK, (c + 1) * FF_CHUNK)
        a = jnp.maximum(_dot(h, wup_ref[:, cs]), 0.0)
        acc = acc + _dot((a * a).astype(BF), wdn_ref[cs, :])
    if gfin_ref is not None:
        acc = _rms(acc, gfin_ref[...])
    return acc


def _split_tail_refs(refs, final):
    n = 4 if final else 3
    return refs[:n], refs[n:]


def _pool_kernel(x_ref, xp_ref, xn_ref, gmix_ref, pw_ref, psc_ref, *rest, tm, seq, final):
    tail, (o_ref, hext_ref) = _split_tail_refs(rest, final)
    i = pl.program_id(1)
    last = pl.num_programs(1) - 1
    g = gmix_ref[...]
    x = x_ref[0]
    h = _rms(x, g)
    hext_ref[0:POOL_HALO, :] = jnp.where(i > 0, _rms(xp_ref[0], g), 0.0)
    hext_ref[POOL_HALO:POOL_HALO + tm, :] = h
    hext_ref[POOL_HALO + tm:2 * POOL_HALO + tm, :] = jnp.where(i < last, _rms(xn_ref[0], g), 0.0)
    t = i * tm + lax.broadcasted_iota(jnp.int32, (tm, 1), 0)
    ys = []
    for gi, win in enumerate(POOL_WINDOWS):
        half = win // 2
        cols = slice(gi * POOL_GROUP, (gi + 1) * POOL_GROUP)
        s = hext_ref[POOL_HALO - half:POOL_HALO - half + tm, cols]
        for j in range(1, win):
            s = s + hext_ref[POOL_HALO - half + j:POOL_HALO - half + j + tm, cols]
        cnt = (jnp.minimum(t + half, seq) - jnp.maximum(t - half, 0)).astype(F32)
        d = s / cnt - h[:, cols]
        ys.append(_dot(d.astype(BF), pw_ref[gi]))
    y = jnp.concatenate(ys, axis=1) * psc_ref[...]
    o_ref[0] = _mlp_tail(x + y, *tail, *([] if final else [None]))


def _conv_kernel(x_ref, xp_ref, xn_ref, gmix_ref, win_ref, bin_ref, dw_ref, dwb_ref, lng_ref, lnb_ref,
                 wout_ref, bout_ref, *rest, tm, final):
    tail, (o_ref, xext_ref, gext_ref) = _split_tail_refs(rest, final)
    i = pl.program_id(1)
    last = pl.num_programs(1) - 1
    ext = tm + 2 * CONV_HALO
    x = x_ref[0]
    xext_ref[0:CONV_HALO, :] = xp_ref[0]
    xext_ref[CONV_HALO:CONV_HALO + tm, :] = x
    xext_ref[CONV_HALO + tm:ext, :] = xn_ref[0]
    hext = _rms(xext_ref[...], gmix_ref[...]).astype(BF)
    a = _dot(hext, win_ref[...]) + bin_ref[...]
    glu = a[:, :D] * jax.nn.sigmoid(a[:, D:])
    r = lax.broadcasted_iota(jnp.int32, (ext, 1), 0)
    inside = jnp.logical_and(jnp.logical_or(r >= CONV_HALO, i > 0),
                             jnp.logical_or(r < CONV_HALO + tm, i < last))
    gext_ref[...] = jnp.where(inside, glu, 0.0)
    off = CONV_HALO - CONV_WIDTH // 2
    c = gext_ref[off:off + tm, :] * dw_ref[0:1, :]
    for k in range(1, CONV_WIDTH):
        c = c + gext_ref[off + k:off + k + tm, :] * dw_ref[k:k + 1, :]
    c = c + dwb_ref[...]
    c = jax.nn.silu(_layer_norm(c, lng_ref[...], lnb_ref[...]))
    y = _dot(c.astype(BF), wout_ref[...]) + bout_ref[...]
    o_ref[0] = _mlp_tail(x + y, *tail, *([] if final else [None]))


def _sgu_kernel(x_ref, gmix_ref, win_ref, bin_ref, lng_ref, lnb_ref, ws_ref, bsb_ref, wout_ref, bout_ref,
                *rest, tm, final):
    tail, (o_ref,) = _split_tail_refs(rest, final)
    x = x_ref[0]
    h = _rms(x, gmix_ref[...]).astype(BF)
    z = jax.nn.gelu(_dot(h, win_ref[...]) + bin_ref[...])
    u = z[:, :D]
    v = _layer_norm(z[:, D:], lng_ref[...], lnb_ref[...]).astype(BF)
    rows = []
    for c in range(tm // SGU_CHUNK):
        heads = []
        for hh in range(SGU_HEADS):
            blk = v[c * SGU_CHUNK:(c + 1) * SGU_CHUNK, hh * SGU_HEAD_DIM:(hh + 1) * SGU_HEAD_DIM]
            heads.append(_dot(ws_ref[hh], blk) + bsb_ref[hh])
        rows.append(jnp.concatenate(heads, axis=1))
    vm = jnp.concatenate(rows, axis=0)
    y = _dot((u * vm).astype(BF), wout_ref[...]) + bout_ref[...]
    o_ref[0] = _mlp_tail(x + y, *tail, *([] if final else [None]))


def _fourier1_kernel(x_ref, gmix_ref, g1_ref, twc_ref, tws_ref, yr_ref, yi_ref, *, bd):
    g = gmix_ref[...]
    for j in range(8):
        h = _rms(x_ref[0, :, j, :], g).astype(BF)
        y = _dot(g1_ref[...], h)
        yc, ys = y[:bd], y[bd:]
        tc = jnp.concatenate([twc_ref[j]] * (D // 128), axis=1)
        ts = jnp.concatenate([tws_ref[j]] * (D // 128), axis=1)
        yr_ref[0, j] = (yc * tc - ys * ts).astype(BF)
        yi_ref[0, j] = (-(yc * ts + ys * tc)).astype(BF)


def _fourier2_kernel(x_ref, yr_ref, yi_ref, p_ref, cc_ref, sc_ref, wout_ref, bout_ref, *rest, nb, ar, final):
    tail, (o_ref,) = _split_tail_refs(rest, final)
    qr, qi = [], []
    for bb in range(nb):
        ycat = jnp.concatenate([yr_ref[bb].reshape(ar, D), yi_ref[bb].reshape(ar, D)], axis=0)
        q = _dot(p_ref[...], ycat)
        qr.append(q[:ar])
        qi.append(q[ar:])
    qr = jnp.concatenate(qr, axis=0).astype(BF)
    qi = jnp.concatenate(qi, axis=0).astype(BF)
    fs = []
    for gi in range(D // FNET_GROUP):
        cols = slice(gi * FNET_GROUP, (gi + 1) * FNET_GROUP)
        fs.append(_dot(qr[:, cols], cc_ref[...]) + _dot(qi[:, cols], sc_ref[...]))
    f = jnp.concatenate(fs, axis=1)
    y = _dot(f.astype(BF), wout_ref[...]) + bout_ref[...]
    x = x_ref[...].reshape(nb * ar, D)
    out = _mlp_tail(x + y, *tail, *([] if final else [None]))
    o_ref[...] = out.reshape(o_ref.shape)


def _resident(shape):
    return pl.BlockSpec(shape, lambda *_: (0,) * len(shape), pipeline_mode=pl.Buffered(1))


def _params(n_axes):
    return pltpu.CompilerParams(dimension_semantics=("arbitrary",) * n_axes,
                                vmem_limit_bytes=V7X_VMEM_LIMIT)


def _row(v):
    return v.reshape(1, -1).astype(F32)


def _tail_operands(gmlp, wup, wdn, gfin):
    ops = [_row(gmlp), wup, wdn]
    specs = [_resident((1, D)), _resident((D, D_FF)), _resident((D_FF, D))]
    if gfin is not None:
        ops.append(_row(gfin))
        specs.append(_resident((1, D)))
    return ops, specs


def _halo_specs(tm, seq, halo):
    per = tm // halo
    nh = seq // halo
    main = pl.BlockSpec((1, tm, D), lambda b, i: (b, i, 0))
    prev = pl.BlockSpec((1, halo, D), lambda b, i: (b, jnp.maximum(i * per - 1, 0), 0))
    nxt = pl.BlockSpec((1, halo, D), lambda b, i: (b, jnp.minimum((i + 1) * per, nh - 1), 0))
    return [main, prev, nxt]


def _pool_layer(x, gmix, pool_w, pool_scale, gmlp, wup, wdn, gfin):
    bsz, seq, _ = x.shape
    tm = min(ROW_TILE, seq)
    tail_ops, tail_specs = _tail_operands(gmlp, wup, wdn, gfin)
    kern = functools.partial(_pool_kernel, tm=tm, seq=seq, final=gfin is not None)
    return pl.pallas_call(
        kern,
        grid=(bsz, seq // tm),
        in_specs=_halo_specs(tm, seq, POOL_HALO) + [
            _resident((1, D)), _resident(pool_w.shape), _resident((1, D))] + tail_specs,
        out_specs=pl.BlockSpec((1, tm, D), lambda b, i: (b, i, 0)),
        out_shape=jax.ShapeDtypeStruct(x.shape, F32),
        scratch_shapes=[pltpu.VMEM((tm + 2 * POOL_HALO, D), F32)],
        compiler_params=_params(2),
        name="pool_mlp",
    )(x, x, x, _row(gmix), pool_w.astype(BF), _row(pool_scale), *tail_ops)


def _conv_layer(x, gmix, w_in, b_in, dw, dw_b, ln_g, ln_b, w_out, b_out, gmlp, wup, wdn, gfin):
    bsz, seq, _ = x.shape
    tm = min(ROW_TILE, seq)
    tail_ops, tail_specs = _tail_operands(gmlp, wup, wdn, gfin)
    dw_pad = jnp.zeros((CONV_WIDTH + 1, D), F32).at[:CONV_WIDTH].set(dw)
    kern = functools.partial(_conv_kernel, tm=tm, final=gfin is not None)
    return pl.pallas_call(
        kern,
        grid=(bsz, seq // tm),
        in_specs=_halo_specs(tm, seq, CONV_HALO) + [
            _resident((1, D)), _resident((D, 2 * D)), _resident((1, 2 * D)),
            _resident((CONV_WIDTH + 1, D)), _resident((1, D)), _resident((1, D)), _resident((1, D)),
            _resident((D, D)), _resident((1, D))] + tail_specs,
        out_specs=pl.BlockSpec((1, tm, D), lambda b, i: (b, i, 0)),
        out_shape=jax.ShapeDtypeStruct(x.shape, F32),
        scratch_shapes=[pltpu.VMEM((tm + 2 * CONV_HALO, D), F32),
                        pltpu.VMEM((tm + 2 * CONV_HALO, D), F32)],
        compiler_params=_params(2),
        name="conv_mlp",
    )(x, x, x, _row(gmix), w_in.astype(BF), _row(b_in), dw_pad, _row(dw_b), _row(ln_g), _row(ln_b),
      w_out.astype(BF), _row(b_out), *tail_ops)


def _sgu_layer(x, gmix, w_in, b_in, ln_g, ln_b, ws, bs, w_out, b_out, gmlp, wup, wdn, gfin):
    bsz, seq, _ = x.shape
    tm = min(ROW_TILE, seq)
    tail_ops, tail_specs = _tail_operands(gmlp, wup, wdn, gfin)
    bs_b = jnp.broadcast_to(bs.astype(F32)[:, :, None], (SGU_HEADS, SGU_CHUNK, SGU_HEAD_DIM))
    kern = functools.partial(_sgu_kernel, tm=tm, final=gfin is not None)
    return pl.pallas_call(
        kern,
        grid=(bsz, seq // tm),
        in_specs=[pl.BlockSpec((1, tm, D), lambda b, i: (b, i, 0)),
                  _resident((1, D)), _resident((D, 2 * D)), _resident((1, 2 * D)),
                  _resident((1, D)), _resident((1, D)),
                  _resident((SGU_HEADS, SGU_CHUNK, SGU_CHUNK)),
                  _resident((SGU_HEADS, SGU_CHUNK, SGU_HEAD_DIM)),
                  _resident((D, D)), _resident((1, D))] + tail_specs,
        out_specs=pl.BlockSpec((1, tm, D), lambda b, i: (b, i, 0)),
        out_shape=jax.ShapeDtypeStruct(x.shape, F32),
        compiler_params=_params(2),
        name="sgu_mlp",
    )(x, _row(gmix), w_in.astype(BF), _row(b_in), _row(ln_g), _row(ln_b), ws.astype(BF), bs_b,
      w_out.astype(BF), _row(b_out), *tail_ops)


def _dft_tables(seq):
    bd = FNET_BD
    a_len = seq // bd
    r = FNET_R

    def cs(n, rows, cols):
        ang = 2.0 * np.pi * ((np.outer(rows, cols)) % n) / n
        return np.cos(ang), np.sin(ang)

    c1, s1 = cs(bd, np.arange(bd), np.arange(bd))
    g1 = np.concatenate([c1, s1], axis=0) / np.sqrt(bd)
    tc, ts = cs(seq, np.arange(a_len), np.arange(bd))
    twc = np.broadcast_to(tc[:, :, None], (a_len, bd, 128))
    tws = np.broadcast_to(ts[:, :, None], (a_len, bd, 128))
    c2, s2 = cs(a_len, np.arange(a_len), np.arange(a_len))
    eye = np.eye(r)
    kc, ks = np.kron(c2, eye), np.kron(s2, eye)
    p = np.block([[kc, ks], [-ks, kc]]) / np.sqrt(a_len)
    cc, sc = cs(FNET_GROUP, np.arange(FNET_GROUP), np.arange(FNET_GROUP))
    cc, sc = cc / np.sqrt(FNET_GROUP), sc / np.sqrt(FNET_GROUP)
    as_bf = lambda m: jnp.asarray(m, dtype=F32).astype(BF)
    return (as_bf(g1), jnp.asarray(twc, dtype=F32), jnp.asarray(tws, dtype=F32),
            as_bf(p), as_bf(cc), as_bf(sc))


def _fourier_layer(x, gmix, w_out, b_out, gmlp, wup, wdn, gfin):
    bsz, seq, _ = x.shape
    bd, r = FNET_BD, FNET_R
    a_len = seq // bd
    ar = a_len * r
    nb = max(1, min(bsz, ROW_TILE // ar))
    g1, twc, tws, p, cc, sc = _dft_tables(seq)
    yshape = jax.ShapeDtypeStruct((bsz, a_len, bd, D), BF)
    yr, yi = pl.pallas_call(
        functools.partial(_fourier1_kernel, bd=bd),
        grid=(a_len // 8, bsz),
        in_specs=[pl.BlockSpec((1, bd, 8, D), lambda j, b: (b, 0, j, 0)),
                  _resident((1, D)), _resident((2 * bd, bd)),
                  pl.BlockSpec((8, bd, 128), lambda j, b: (j, 0, 0)),
                  pl.BlockSpec((8, bd, 128), lambda j, b: (j, 0, 0))],
        out_specs=[pl.BlockSpec((1, 8, bd, D), lambda j, b: (b, j, 0, 0)),
                   pl.BlockSpec((1, 8, bd, D), lambda j, b: (b, j, 0, 0))],
        out_shape=[yshape, yshape],
        compiler_params=_params(2),
        name="fourier_stage1",
    )(x.reshape(bsz, bd, a_len, D), _row(gmix), g1, twc, tws)

    tail_ops, tail_specs = _tail_operands(gmlp, wup, wdn, gfin)
    blk = pl.BlockSpec((nb, a_len, r, D), lambda b, k: (b, 0, k, 0))
    out = pl.pallas_call(
        functools.partial(_fourier2_kernel, nb=nb, ar=ar, final=gfin is not None),
        grid=(bsz // nb, bd // r),
        in_specs=[blk, blk, blk, _resident((2 * ar, 2 * ar)),
                  _resident((FNET_GROUP, FNET_GROUP)), _resident((FNET_GROUP, FNET_GROUP)),
                  _resident((D, D)), _resident((1, D))] + tail_specs,
        out_specs=blk,
        out_shape=jax.ShapeDtypeStruct((bsz, a_len, bd, D), F32),
        compiler_params=_params(2),
        name="fourier_stage2_mlp",
    )(x.reshape(bsz, a_len, bd, D), yr, yi, p, cc, sc, w_out.astype(BF), _row(b_out), *tail_ops)
    return out.reshape(bsz, seq, D)


def _trunk(x, norm_mix_g, norm_mlp_g, w_up, w_down, final_norm_g, mixers):
    depth = norm_mix_g.shape[0]
    for i in range(depth):
        kind, j = i % 4, i // 4
        gfin = final_norm_g if i == depth - 1 else None
        common = (norm_mlp_g[i], w_up[i], w_down[i], gfin)
        x = mixers[kind](x, norm_mix_g[i], j, *common)
    return x


def kernel(x_prompt, x_sample, norm_mix_g, norm_mlp_g, w_up, w_down, final_norm_g, pool_w, pool_scale,
           conv_w_in, conv_b_in, conv_dw, conv_dw_b, conv_ln_g, conv_ln_b, conv_w_out, conv_b_out,
           sgu_w_in, sgu_b_in, sgu_ln_g, sgu_ln_b, sgu_ws, sgu_bs, sgu_w_out, sgu_b_out,
           fnet_w_out, fnet_b_out):
    w_up = w_up.astype(BF)
    w_down = w_down.astype(BF)

    def pool(x, g, j, *common):
        return _pool_layer(x, g, pool_w[j], pool_scale[j], *common)

    def conv(x, g, j, *common):
        return _conv_layer(x, g, conv_w_in[j], conv_b_in[j], conv_dw[j], conv_dw_b[j], conv_ln_g[j],
                           conv_ln_b[j], conv_w_out[j], conv_b_out[j], *common)

    def sgu(x, g, j, *common):
        return _sgu_layer(x, g, sgu_w_in[j], sgu_b_in[j], sgu_ln_g[j], sgu_ln_b[j], sgu_ws[j], sgu_bs[j],
                          sgu_w_out[j], sgu_b_out[j], *common)

    def fourier(x, g, j, *common):
        return _fourier_layer(x, g, fnet_w_out[j], fnet_b_out[j], *common)

    mixers = (pool, conv, sgu, fourier)
    y_prompt = _trunk(x_prompt, norm_mix_g, norm_mlp_g, w_up, w_down, final_norm_g, mixers)
    y_sample = _trunk(x_sample, norm_mix_g, norm_mlp_g, w_up, w_down, final_norm_g, mixers)
    return (y_prompt, y_sample)
```

```python
import functools

import numpy as np
import jax
import jax.numpy as jnp
from jax import lax
from jax.experimental import pallas as pl
from jax.experimental.pallas import tpu as pltpu

D = 1024
D_FF = 4 * D
EPS = 1e-6
POOL_WINDOWS = (2, 4, 8, 16)
POOL_GROUP = D // len(POOL_WINDOWS)
POOL_HALO = 8
CONV_WIDTH = 31
CONV_HALO = 16
SGU_HEADS = 4
SGU_HEAD_DIM = D // SGU_HEADS
SGU_CHUNK = 128
FNET_GROUP = 256
FNET_BD = 256
FNET_R = 16
SUBLANES = 8
MXU_COLS = 256

FF_CHUNK = 1024
ROW_TILE = 512
V7X_VMEM_LIMIT = 56 * 1024 * 1024

BF = jnp.bfloat16
F32 = jnp.float32


def _rms(x, g):
    return x * lax.rsqrt(jnp.mean(x * x, axis=-1, keepdims=True) + EPS) * g


def _layer_norm(x, g, b):
    mu = jnp.mean(x, axis=-1, keepdims=True)
    xc = x - mu
    var = jnp.mean(xc * xc, axis=-1, keepdims=True)
    return xc * lax.rsqrt(var + EPS) * g + b


def _dot(a, b):
    return jnp.dot(a, b, preferred_element_type=F32)


def _mlp(load_h, load_x1, wup_ref, wdn_ref, gfin_ref, side_work=()):
    per_chunk = FF_CHUNK // MXU_COLS
    n_out = D // MXU_COLS
    n_slots = (D_FF // FF_CHUNK) * (per_chunk + n_out)
    slot = [0]

    def run_side():
        lo = slot[0] * len(side_work) // n_slots
        slot[0] += 1
        for piece in side_work[lo:slot[0] * len(side_work) // n_slots]:
            piece()

    acc = [None] * n_out
    for c in range(D_FF // FF_CHUNK):
        parts = []
        for n in range(per_chunk):
            lo = c * FF_CHUNK + n * MXU_COLS
            a = jnp.maximum(_dot(load_h(), wup_ref[:, lo:lo + MXU_COLS]), 0.0)
            parts.append((a * a).astype(BF))
            run_side()
        a = jnp.concatenate(parts, axis=1)
        for n in range(n_out):
            d = _dot(a, wdn_ref[c * FF_CHUNK:(c + 1) * FF_CHUNK, n * MXU_COLS:(n + 1) * MXU_COLS])
            acc[n] = d if acc[n] is None else acc[n] + d
            run_side()
    out = load_x1() + jnp.concatenate(acc, axis=1)
    if gfin_ref is not None:
        out = _rms(out, gfin_ref[...])
    return out


def _split_tail_refs(refs, final):
    n = 4 if final else 3
    return tuple(refs[:n]) + (() if final else (None,)), refs[n:]


def _skewed_mlp(step, tail, o_ref, x1_scr, h_scr, side_work=()):
    @pl.when(step == 0)
    def _():
        x1_scr[...] = jnp.zeros_like(x1_scr)
        h_scr[...] = jnp.zeros_like(h_scr)

    _, wup_ref, wdn_ref, gfin_ref = tail
    o_ref[0] = _mlp(lambda: h_scr[...], lambda: x1_scr[...], wup_ref, wdn_ref, gfin_ref, side_work)


def _stash(x1, tail, x1_scr, h_scr):
    x1_scr[...] = x1
    h_scr[...] = _rms(x1, tail[0][...]).astype(BF)


def _seq_tile(step, n_tiles, tiles_per_seq):
    return jnp.minimum(step, n_tiles - 1) % tiles_per_seq


def _pool_kernel(x_ref, xp_ref, xn_ref, gmix_ref, pw_ref, psc_ref, *rest, tm, seq, n_tiles, final):
    tail, (o_ref, hext_ref, x1_scr, h_scr) = _split_tail_refs(rest, final)
    step = pl.program_id(0)
    i = _seq_tile(step, n_tiles, seq // tm)
    last = seq // tm - 1
    ys = []

    def norm_rows():
        g = gmix_ref[...]
        hext_ref[0:POOL_HALO, :] = jnp.where(i > 0, _rms(xp_ref[0], g), 0.0)
        hext_ref[POOL_HALO:POOL_HALO + tm, :] = _rms(x_ref[0], g)
        hext_ref[POOL_HALO + tm:2 * POOL_HALO + tm, :] = jnp.where(i < last, _rms(xn_ref[0], g), 0.0)

    def group(gi, win):
        half = win // 2
        cols = slice(gi * POOL_GROUP, (gi + 1) * POOL_GROUP)
        t = i * tm + lax.broadcasted_iota(jnp.int32, (tm, 1), 0)
        s = hext_ref[POOL_HALO - half:POOL_HALO - half + tm, cols]
        for j in range(1, win):
            s = s + hext_ref[POOL_HALO - half + j:POOL_HALO - half + j + tm, cols]
        cnt = (jnp.minimum(t + half, seq) - jnp.maximum(t - half, 0)).astype(F32)
        d = s / cnt - hext_ref[POOL_HALO:POOL_HALO + tm, cols]
        ys.append(_dot(d.astype(BF), pw_ref[gi]))

    pieces = [norm_rows] + [functools.partial(group, gi, win) for gi, win in enumerate(POOL_WINDOWS)]
    _skewed_mlp(step, tail, o_ref, x1_scr, h_scr, pieces)
    y = jnp.concatenate(ys, axis=1) * psc_ref[...]
    _stash(x_ref[0] + y, tail, x1_scr, h_scr)


def _conv_kernel(x_ref, xp_ref, xn_ref, gmix_ref, win_ref, bin_ref, dw_ref, dwb_ref, lng_ref, lnb_ref,
                 wout_ref, bout_ref, *rest, tm, seq, n_tiles, final):
    tail, (o_ref, hext_ref, gext_ref, part_ref, x1_scr, h_scr) = _split_tail_refs(rest, final)
    step = pl.program_id(0)
    i = _seq_tile(step, n_tiles, seq // tm)
    last = seq // tm - 1
    ext = tm + 2 * CONV_HALO
    base = CONV_HALO - CONV_WIDTH // 2
    n_blk = D // MXU_COLS
    acc = [None] * n_blk
    ys = []

    def norm_rows():
        g = gmix_ref[...]
        hext_ref[0:CONV_HALO, :] = _rms(xp_ref[0], g).astype(BF)
        hext_ref[CONV_HALO:CONV_HALO + tm, :] = _rms(x_ref[0], g).astype(BF)
        hext_ref[CONV_HALO + tm:ext, :] = _rms(xn_ref[0], g).astype(BF)

    def gated_cols(cb):
        cols = slice(cb * MXU_COLS, (cb + 1) * MXU_COLS)
        gcols = slice(D + cb * MXU_COLS, D + (cb + 1) * MXU_COLS)
        val = _dot(hext_ref[...], win_ref[:, cols]) + bin_ref[:, cols]
        gate = _dot(hext_ref[...], win_ref[:, gcols]) + bin_ref[:, gcols]
        r = lax.broadcasted_iota(jnp.int32, (ext, 1), 0)
        inside = jnp.logical_and(jnp.logical_or(r >= CONV_HALO, i > 0),
                                 jnp.logical_or(r < CONV_HALO + tm, i < last))
        gext_ref[:, cols] = jnp.where(inside, val * jax.nn.sigmoid(gate), 0.0)

    def taps(cb, mis):
        cols = slice(cb * MXU_COLS, (cb + 1) * MXU_COLS)
        part = None
        for q in range((base + CONV_WIDTH - 1) // SUBLANES + 1):
            k = SUBLANES * q + mis - base
            if 0 <= k < CONV_WIDTH:
                term = gext_ref[SUBLANES * q:SUBLANES * q + tm + SUBLANES, cols] * dw_ref[k:k + 1, cols]
                part = term if part is None else part + term
        if mis == 0:
            shifted = part[0:tm]
        else:
            part_ref[mis % 2, :, cols] = part
            shifted = part_ref[mis % 2, mis:mis + tm, cols]
        acc[cb] = shifted if acc[cb] is None else acc[cb] + shifted

    def activate():
        c = jnp.concatenate(acc, axis=1) + dwb_ref[...]
        hext_ref[0:tm, :] = jax.nn.silu(_layer_norm(c, lng_ref[...], lnb_ref[...])).astype(BF)

    def project(cb):
        cols = slice(cb * MXU_COLS, (cb + 1) * MXU_COLS)
        ys.append(_dot(hext_ref[0:tm, :], wout_ref[:, cols]) + bout_ref[:, cols])

    pieces = ([norm_rows] + [functools.partial(gated_cols, cb) for cb in range(n_blk)]
              + [functools.partial(taps, cb, mis) for cb in range(n_blk) for mis in range(SUBLANES)]
              + [activate] + [functools.partial(project, cb) for cb in range(n_blk)])
    _skewed_mlp(step, tail, o_ref, x1_scr, h_scr, pieces)
    _stash(x_ref[0] + jnp.concatenate(ys, axis=1), tail, x1_scr, h_scr)


def _sgu_kernel(x_ref, gmix_ref, win_ref, bin_ref, lng_ref, lnb_ref, ws_ref, bsb_ref, wout_ref, bout_ref,
                *rest, tm, final):
    tail, (o_ref, x1_scr, h_scr) = _split_tail_refs(rest, final)
    vals = {}
    rows = []

    def gate_and_norm():
        h = _rms(x_ref[0], gmix_ref[...]).astype(BF)
        z = jax.nn.gelu(_dot(h, win_ref[...]) + bin_ref[...])
        vals["u"] = z[:, :D]
        vals["v"] = _layer_norm(z[:, D:], lng_ref[...], lnb_ref[...]).astype(BF)

    def mix_chunk(c):
        heads = []
        for hh in range(SGU_HEADS):
            blk = vals["v"][c * SGU_CHUNK:(c + 1) * SGU_CHUNK, hh * SGU_HEAD_DIM:(hh + 1) * SGU_HEAD_DIM]
            heads.append(_dot(ws_ref[hh], blk) + bsb_ref[hh])
        rows.append(jnp.concatenate(heads, axis=1))

    def project():
        vm = jnp.concatenate(rows, axis=0)
        vals["y"] = _dot((vals["u"] * vm).astype(BF), wout_ref[...]) + bout_ref[...]

    pieces = [gate_and_norm] + [functools.partial(mix_chunk, c) for c in range(tm // SGU_CHUNK)] + [project]
    _skewed_mlp(pl.program_id(0), tail, o_ref, x1_scr, h_scr, pieces)
    _stash(x_ref[0] + vals["y"], tail, x1_scr, h_scr)


def _fourier1_kernel(x_ref, gmix_ref, g1_ref, twc_ref, tws_ref, yr_ref, yi_ref, *, bd):
    g = gmix_ref[...]
    for j in range(SUBLANES):
        h = _rms(x_ref[0, :, j, :], g).astype(BF)
        y = _dot(g1_ref[...], h)
        yc, ys = y[:bd], y[bd:]
        tc = jnp.concatenate([twc_ref[j]] * (D // 128), axis=1)
        ts = jnp.concatenate([tws_ref[j]] * (D // 128), axis=1)
        yr_ref[0, j] = (yc * tc - ys * ts).astype(BF)
        yi_ref[0, j] = (-(yc * ts + ys * tc)).astype(BF)


def _fourier2_kernel(x_ref, yr_ref, yi_ref, p_ref, cc_ref, sc_ref, wout_ref, bout_ref, *rest, nb, ar, final):
    (gmlp_ref, wup_ref, wdn_ref, gfin_ref), (o_ref,) = _split_tail_refs(rest, final)
    qr, qi = [], []
    for bb in range(nb):
        ycat = jnp.concatenate([yr_ref[bb].reshape(ar, D), yi_ref[bb].reshape(ar, D)], axis=0)
        q = _dot(p_ref[...], ycat)
        qr.append(q[:ar])
        qi.append(q[ar:])
    qr = jnp.concatenate(qr, axis=0).astype(BF)
    qi = jnp.concatenate(qi, axis=0).astype(BF)
    fs = []
    for gi in range(D // FNET_GROUP):
        cols = slice(gi * FNET_GROUP, (gi + 1) * FNET_GROUP)
        fs.append(_dot(qr[:, cols], cc_ref[...]) + _dot(qi[:, cols], sc_ref[...]))
    f = jnp.concatenate(fs, axis=1)
    y = _dot(f.astype(BF), wout_ref[...]) + bout_ref[...]
    x1 = x_ref[...].reshape(nb * ar, D) + y
    h = _rms(x1, gmlp_ref[...]).astype(BF)
    out = _mlp(lambda: h, lambda: x1, wup_ref, wdn_ref, gfin_ref)
    o_ref[...] = out.reshape(o_ref.shape)


def _resident(shape):
    return pl.BlockSpec(shape, lambda *_: (0,) * len(shape), pipeline_mode=pl.Buffered(1))


def _params(n_axes):
    return pltpu.CompilerParams(dimension_semantics=("arbitrary",) * n_axes,
                                vmem_limit_bytes=V7X_VMEM_LIMIT)


def _row(v):
    return v.reshape(1, -1).astype(F32)


def _tail_operands(gmlp, wup, wdn, gfin):
    ops = [_row(gmlp), wup, wdn]
    specs = [_resident((1, D)), _resident((D, D_FF)), _resident((D_FF, D))]
    if gfin is not None:
        ops.append(_row(gfin))
        specs.append(_resident((1, D)))
    return ops, specs


def _skewed_specs(tm, seq, n_tiles, halo):
    per_seq = seq // tm

    def tile(s):
        st = jnp.minimum(s, n_tiles - 1)
        return st // per_seq, st % per_seq

    def main(s):
        b, i = tile(s)
        return b, i, 0

    def out(s):
        so = jnp.maximum(s - 1, 0)
        return so // per_seq, so % per_seq, 0

    specs = [pl.BlockSpec((1, tm, D), main)]
    if halo:
        per = tm // halo
        nh = seq // halo

        def prev(s):
            b, i = tile(s)
            return b, jnp.maximum(i * per - 1, 0), 0

        def nxt(s):
            b, i = tile(s)
            return b, jnp.minimum((i + 1) * per, nh - 1), 0

        specs += [pl.BlockSpec((1, halo, D), prev), pl.BlockSpec((1, halo, D), nxt)]
    return specs, pl.BlockSpec((1, tm, D), out)


def _stash_scratch(tm):
    return [pltpu.VMEM((tm, D), F32), pltpu.VMEM((tm, D), BF)]


def _pool_layer(x, gmix, pool_w, pool_scale, gmlp, wup, wdn, gfin):
    bsz, seq, _ = x.shape
    tm = min(ROW_TILE, seq)
    n_tiles = bsz * (seq // tm)
    tail_ops, tail_specs = _tail_operands(gmlp, wup, wdn, gfin)
    x_specs, out_spec = _skewed_specs(tm, seq, n_tiles, POOL_HALO)
    kern = functools.partial(_pool_kernel, tm=tm, seq=seq, n_tiles=n_tiles, final=gfin is not None)
    return pl.pallas_call(
        kern,
        grid=(n_tiles + 1,),
        in_specs=x_specs + [_resident((1, D)), _resident(pool_w.shape), _resident((1, D))] + tail_specs,
        out_specs=out_spec,
        out_shape=jax.ShapeDtypeStruct(x.shape, F32),
        scratch_shapes=[pltpu.VMEM((tm + 2 * POOL_HALO, D), F32)] + _stash_scratch(tm),
        compiler_params=_params(1),
        name="pool_mlp",
    )(x, x, x, _row(gmix), pool_w.astype(BF), _row(pool_scale), *tail_ops)


def _conv_layer(x, gmix, w_in, b_in, dw, dw_b, ln_g, ln_b, w_out, b_out, gmlp, wup, wdn, gfin):
    bsz, seq, _ = x.shape
    tm = min(ROW_TILE, seq)
    n_tiles = bsz * (seq // tm)
    tail_ops, tail_specs = _tail_operands(gmlp, wup, wdn, gfin)
    x_specs, out_spec = _skewed_specs(tm, seq, n_tiles, CONV_HALO)
    dw_pad = jnp.zeros((CONV_WIDTH + 1, D), F32).at[:CONV_WIDTH].set(dw)
    kern = functools.partial(_conv_kernel, tm=tm, seq=seq, n_tiles=n_tiles, final=gfin is not None)
    ext = tm + 2 * CONV_HALO
    return pl.pallas_call(
        kern,
        grid=(n_tiles + 1,),
        in_specs=x_specs + [
            _resident((1, D)), _resident((D, 2 * D)), _resident((1, 2 * D)),
            _resident((CONV_WIDTH + 1, D)), _resident((1, D)), _resident((1, D)), _resident((1, D)),
            _resident((D, D)), _resident((1, D))] + tail_specs,
        out_specs=out_spec,
        out_shape=jax.ShapeDtypeStruct(x.shape, F32),
        scratch_shapes=[pltpu.VMEM((ext, D), BF), pltpu.VMEM((ext, D), F32),
                        pltpu.VMEM((2, tm + SUBLANES, D), F32)] + _stash_scratch(tm),
        compiler_params=_params(1),
        name="conv_mlp",
    )(x, x, x, _row(gmix), w_in.astype(BF), _row(b_in), dw_pad, _row(dw_b), _row(ln_g), _row(ln_b),
      w_out.astype(BF), _row(b_out), *tail_ops)


def _sgu_layer(x, gmix, w_in, b_in, ln_g, ln_b, ws, bs, w_out, b_out, gmlp, wup, wdn, gfin):
    bsz, seq, _ = x.shape
    tm = min(ROW_TILE, seq)
    n_tiles = bsz * (seq // tm)
    tail_ops, tail_specs = _tail_operands(gmlp, wup, wdn, gfin)
    x_specs, out_spec = _skewed_specs(tm, seq, n_tiles, 0)
    bs_b = jnp.broadcast_to(bs.astype(F32)[:, :, None], (SGU_HEADS, SGU_CHUNK, SGU_HEAD_DIM))
    kern = functools.partial(_sgu_kernel, tm=tm, final=gfin is not None)
    return pl.pallas_call(
        kern,
        grid=(n_tiles + 1,),
        in_specs=x_specs + [
            _resident((1, D)), _resident((D, 2 * D)), _resident((1, 2 * D)),
            _resident((1, D)), _resident((1, D)),
            _resident((SGU_HEADS, SGU_CHUNK, SGU_CHUNK)),
            _resident((SGU_HEADS, SGU_CHUNK, SGU_HEAD_DIM)),
            _resident((D, D)), _resident((1, D))] + tail_specs,
        out_specs=out_spec,
        out_shape=jax.ShapeDtypeStruct(x.shape, F32),
        scratch_shapes=_stash_scratch(tm),
        compiler_params=_params(1),
        name="sgu_mlp",
    )(x, _row(gmix), w_in.astype(BF), _row(b_in), _row(ln_g), _row(ln_b), ws.astype(BF), bs_b,
      w_out.astype(BF), _row(b_out), *tail_ops)


def _dft_tables(seq):
    bd = FNET_BD
    a_len = seq // bd
    r = FNET_R

    def cs(n, rows, cols):
        ang = 2.0 * np.pi * ((np.outer(rows, cols)) % n) / n
        return np.cos(ang), np.sin(ang)

    c1, s1 = cs(bd, np.arange(bd), np.arange(bd))
    g1 = np.concatenate([c1, s1], axis=0) / np.sqrt(bd)
    tc, ts = cs(seq, np.arange(a_len), np.arange(bd))
    twc = np.broadcast_to(tc[:, :, None], (a_len, bd, 128))
    tws = np.broadcast_to(ts[:, :, None], (a_len, bd, 128))
    c2, s2 = cs(a_len, np.arange(a_len), np.arange(a_len))
    eye = np.eye(r)
    kc, ks = np.kron(c2, eye), np.kron(s2, eye)
    p = np.block([[kc, ks], [-ks, kc]]) / np.sqrt(a_len)
    cc, sc = cs(FNET_GROUP, np.arange(FNET_GROUP), np.arange(FNET_GROUP))
    cc, sc = cc / np.sqrt(FNET_GROUP), sc / np.sqrt(FNET_GROUP)
    as_bf = lambda m: jnp.asarray(m, dtype=F32).astype(BF)
    return (as_bf(g1), jnp.asarray(twc, dtype=F32), jnp.asarray(tws, dtype=F32),
            as_bf(p), as_bf(cc), as_bf(sc))


def _fourier_layer(x, gmix, w_out, b_out, gmlp, wup, wdn, gfin):
    bsz, seq, _ = x.shape
    bd, r = FNET_BD, FNET_R
    a_len = seq // bd
    ar = a_len * r
    nb = max(1, min(bsz, ROW_TILE // ar))
    g1, twc, tws, p, cc, sc = _dft_tables(seq)
    yshape = jax.ShapeDtypeStruct((bsz, a_len, bd, D), BF)
    yr, yi = pl.pallas_call(
        functools.partial(_fourier1_kernel, bd=bd),
        grid=(a_len // SUBLANES, bsz),
        in_specs=[pl.BlockSpec((1, bd, SUBLANES, D), lambda j, b: (b, 0, j, 0)),
                  _resident((1, D)), _resident((2 * bd, bd)),
                  pl.BlockSpec((SUBLANES, bd, 128), lambda j, b: (j, 0, 0)),
                  pl.BlockSpec((SUBLANES, bd, 128), lambda j, b: (j, 0, 0))],
        out_specs=[pl.BlockSpec((1, SUBLANES, bd, D), lambda j, b: (b, j, 0, 0)),
                   pl.BlockSpec((1, SUBLANES, bd, D), lambda j, b: (b, j, 0, 0))],
        out_shape=[yshape, yshape],
        compiler_params=_params(2),
        name="fourier_stage1",
    )(x.reshape(bsz, bd, a_len, D), _row(gmix), g1, twc, tws)

    tail_ops, tail_specs = _tail_operands(gmlp, wup, wdn, gfin)
    blk = pl.BlockSpec((nb, a_len, r, D), lambda b, k: (b, 0, k, 0))
    out = pl.pallas_call(
        functools.partial(_fourier2_kernel, nb=nb, ar=ar, final=gfin is not None),
        grid=(bsz // nb, bd // r),
        in_specs=[blk, blk, blk, _resident((2 * ar, 2 * ar)),
                  _resident((FNET_GROUP, FNET_GROUP)), _resident((FNET_GROUP, FNET_GROUP)),
                  _resident((D, D)), _resident((1, D))] + tail_specs,
        out_specs=blk,
        out_shape=jax.ShapeDtypeStruct((bsz, a_len, bd, D), F32),
        compiler_params=_params(2),
        name="fourier_stage2_mlp",
    )(x.reshape(bsz, a_len, bd, D), yr, yi, p, cc, sc, w_out.astype(BF), _row(b_out), *tail_ops)
    return out.reshape(bsz, seq, D)


def _trunk(x, norm_mix_g, norm_mlp_g, w_up, w_down, final_norm_g, mixers):
    depth = norm_mix_g.shape[0]
    for i in range(depth):
        kind, j = i % 4, i // 4
        gfin = final_norm_g if i == depth - 1 else None
        common = (norm_mlp_g[i], w_up[i], w_down[i], gfin)
        x = mixers[kind](x, norm_mix_g[i], j, *common)
    return x


def kernel(x_prompt, x_sample, norm_mix_g, norm_mlp_g, w_up, w_down, final_norm_g, pool_w, pool_scale,
           conv_w_in, conv_b_in, conv_dw, conv_dw_b, conv_ln_g, conv_ln_b, conv_w_out, conv_b_out,
           sgu_w_in, sgu_b_in, sgu_ln_g, sgu_ln_b, sgu_ws, sgu_bs, sgu_w_out, sgu_b_out,
           fnet_w_out, fnet_b_out):
    w_up = w_up.astype(BF)
    w_down = w_down.astype(BF)

    def pool(x, g, j, *common):
        return _pool_layer(x, g, pool_w[j], pool_scale[j], *common)

    def conv(x, g, j, *common):
        return _conv_layer(x, g, conv_w_in[j], conv_b_in[j], conv_dw[j], conv_dw_b[j], conv_ln_g[j],
                           conv_ln_b[j], conv_w_out[j], conv_b_out[j], *common)

    def sgu(x, g, j, *common):
        return _sgu_layer(x, g, sgu_w_in[j], sgu_b_in[j], sgu_ln_g[j], sgu_ln_b[j], sgu_ws[j], sgu_bs[j],
                          sgu_w_out[j], sgu_b_out[j], *common)

    def fourier(x, g, j, *common):
        return _fourier_layer(x, g, fnet_w_out[j], fnet_b_out[j], *common)

    mixers = (pool, conv, sgu, fourier)
    y_prompt = _trunk(x_prompt, norm_mix_g, norm_mlp_g, w_up, w_down, final_norm_g, mixers)
    y_sample = _trunk(x_sample, norm_mix_g, norm_mlp_g, w_up, w_down, final_norm_g, mixers)
    return (y_prompt, y_sample)
```

```python
import functools

import numpy as np
import jax
import jax.numpy as jnp
from jax import lax
from jax.experimental import pallas as pl
from jax.experimental.pallas import tpu as pltpu

D = 1024
D_FF = 4 * D
EPS = 1e-6
POOL_WINDOWS = (2, 4, 8, 16)
POOL_GROUP = D // len(POOL_WINDOWS)
POOL_HALO = 8
CONV_WIDTH = 31
CONV_HALO = 16
SGU_HEADS = 4
SGU_HEAD_DIM = D // SGU_HEADS
SGU_CHUNK = 128
FNET_GROUP = 256
FNET_BD = 256
FNET_R = 16
SUBLANES = 8
MXU_COLS = 256

FF_CHUNK = 1024
ROW_TILE = 512
V7X_VMEM_LIMIT = 56 * 1024 * 1024
MLP_SLOTS = (D_FF // FF_CHUNK) * (FF_CHUNK // MXU_COLS + D // MXU_COLS)

BF = jnp.bfloat16
F32 = jnp.float32


def _rms(x, g):
    return x * lax.rsqrt(jnp.mean(x * x, axis=-1, keepdims=True) + EPS) * g


def _layer_norm(x, g, b):
    mu = jnp.mean(x, axis=-1, keepdims=True)
    xc = x - mu
    var = jnp.mean(xc * xc, axis=-1, keepdims=True)
    return xc * lax.rsqrt(var + EPS) * g + b


def _dot(a, b):
    return jnp.dot(a, b, preferred_element_type=F32)


def _mlp(load_h, load_x1, wup_ref, wdn_ref, gfin_ref, side_work=()):
    per_chunk = FF_CHUNK // MXU_COLS
    n_out = D // MXU_COLS
    slot = [0]

    def run_side(result):
        pieces = side_work[slot[0]] if side_work else ()
        slot[0] += 1
        if pieces:
            pin = jnp.concatenate([result[-1:, -128:] * 0.0] * (D // 128), axis=1)
            for piece in pieces:
                piece(pin)

    acc = [None] * n_out
    for c in range(D_FF // FF_CHUNK):
        parts = []
        for n in range(per_chunk):
            lo = c * FF_CHUNK + n * MXU_COLS
            a = _dot(load_h(), wup_ref[:, lo:lo + MXU_COLS])
            ab = jnp.maximum(a.astype(BF), 0.0)
            parts.append(ab * ab)
            run_side(a)
        a = jnp.concatenate(parts, axis=1)
        for n in range(n_out):
            d = _dot(a, wdn_ref[c * FF_CHUNK:(c + 1) * FF_CHUNK, n * MXU_COLS:(n + 1) * MXU_COLS])
            acc[n] = d if acc[n] is None else acc[n] + d
            run_side(d)
    out = load_x1() + jnp.concatenate(acc, axis=1)
    if gfin_ref is not None:
        out = _rms(out, gfin_ref[...])
    return out


def _spread(pieces, first, last):
    slots = [[] for _ in range(MLP_SLOTS)]
    for idx, piece in enumerate(pieces):
        slots[first + idx * (last - first + 1) // len(pieces)].append(piece)
    return slots


def _split_tail_refs(refs, final):
    n = 4 if final else 3
    return tuple(refs[:n]) + (() if final else (None,)), refs[n:]


def _skewed_mlp(step, tail, o_ref, x1_scr, h_scr, side_work=()):
    @pl.when(step == 0)
    def _():
        x1_scr[...] = jnp.zeros_like(x1_scr)
        h_scr[...] = jnp.zeros_like(h_scr)

    _, wup_ref, wdn_ref, gfin_ref = tail
    o_ref[0] = _mlp(lambda: h_scr[...], lambda: x1_scr[...], wup_ref, wdn_ref, gfin_ref, side_work)


def _stash(x1, tail, x1_scr, h_scr):
    x1_scr[...] = x1
    h_scr[...] = _rms(x1, tail[0][...]).astype(BF)


def _seq_tile(step, n_tiles, tiles_per_seq):
    return jnp.minimum(step, n_tiles - 1) % tiles_per_seq


def _pool_kernel(x_ref, xp_ref, xn_ref, gmix_ref, pw_ref, psc_ref, *rest, tm, seq, n_tiles, final):
    tail, (o_ref, hext_ref, x1_scr, h_scr) = _split_tail_refs(rest, final)
    step = pl.program_id(0)
    i = _seq_tile(step, n_tiles, seq // tm)
    last = seq // tm - 1
    ys = []

    def norm_rows(pin):
        g = gmix_ref[...]
        hext_ref[0:POOL_HALO, :] = jnp.where(i > 0, _rms(xp_ref[0], g), 0.0)
        hext_ref[POOL_HALO:POOL_HALO + tm, :] = _rms(x_ref[0], g)
        hext_ref[POOL_HALO + tm:2 * POOL_HALO + tm, :] = jnp.where(i < last, _rms(xn_ref[0], g), 0.0)

    def group(gi, win, pin):
        half = win // 2
        cols = slice(gi * POOL_GROUP, (gi + 1) * POOL_GROUP)
        t = i * tm + lax.broadcasted_iota(jnp.int32, (tm, 1), 0)
        s = hext_ref[POOL_HALO - half:POOL_HALO - half + tm, cols]
        for j in range(1, win):
            s = s + hext_ref[POOL_HALO - half + j:POOL_HALO - half + j + tm, cols]
        cnt = (jnp.minimum(t + half, seq) - jnp.maximum(t - half, 0)).astype(F32)
        d = s / cnt - hext_ref[POOL_HALO:POOL_HALO + tm, cols]
        ys.append(_dot(d.astype(BF), pw_ref[gi]))

    pieces = [norm_rows] + [functools.partial(group, gi, win) for gi, win in enumerate(POOL_WINDOWS)]
    _skewed_mlp(step, tail, o_ref, x1_scr, h_scr, _spread(pieces, 0, MLP_SLOTS - 1))
    y = jnp.concatenate(ys, axis=1) * psc_ref[...]
    _stash(x_ref[0] + y, tail, x1_scr, h_scr)


def _conv_kernel(x_ref, xp_ref, xn_ref, gmix_ref, win_ref, bin_ref, dw_ref, dwb_ref, lng_ref, lnb_ref,
                 wout_ref, bout_ref, *rest, tm, seq, n_tiles, final):
    tail, (o_ref, hext_ref, gext_ref, part_ref, x1_scr, h_scr) = _split_tail_refs(rest, final)
    step = pl.program_id(0)
    i = _seq_tile(step, n_tiles, seq // tm)
    last = seq // tm - 1
    ext = tm + 2 * CONV_HALO
    base = CONV_HALO - CONV_WIDTH // 2
    n_blk = D // MXU_COLS
    acc = [None] * n_blk
    ys = []

    def norm_rows(pin):
        g = gmix_ref[...]
        hext_ref[0:CONV_HALO, :] = _rms(xp_ref[0], g).astype(BF)
        hext_ref[CONV_HALO:CONV_HALO + tm, :] = _rms(x_ref[0], g).astype(BF)
        hext_ref[CONV_HALO + tm:ext, :] = _rms(xn_ref[0], g).astype(BF)

    def gated_cols(cb, pin):
        cols = slice(cb * MXU_COLS, (cb + 1) * MXU_COLS)
        gcols = slice(D + cb * MXU_COLS, D + (cb + 1) * MXU_COLS)
        val = _dot(hext_ref[...], win_ref[:, cols]) + bin_ref[:, cols]
        gate = _dot(hext_ref[...], win_ref[:, gcols]) + bin_ref[:, gcols]
        r = lax.broadcasted_iota(jnp.int32, (ext, 1), 0)
        inside = jnp.logical_and(jnp.logical_or(r >= CONV_HALO, i > 0),
                                 jnp.logical_or(r < CONV_HALO + tm, i < last))
        gext_ref[:, cols] = jnp.where(inside, val * jax.nn.sigmoid(gate), 0.0)

    def taps(cb, mis, pin):
        cols = slice(cb * MXU_COLS, (cb + 1) * MXU_COLS)
        part = None
        for q in range((base + CONV_WIDTH - 1) // SUBLANES + 1):
            k = SUBLANES * q + mis - base
            if 0 <= k < CONV_WIDTH:
                term = (gext_ref[SUBLANES * q:SUBLANES * q + tm + SUBLANES, cols]
                        * (dw_ref[k:k + 1, cols] + pin[:, cols]))
                part = term if part is None else part + term
        if mis == 0:
            shifted = part[0:tm]
        else:
            part_ref[mis % 2, :, cols] = part
            shifted = part_ref[mis % 2, mis:mis + tm, cols]
        acc[cb] = shifted if acc[cb] is None else acc[cb] + shifted

    def activate(pin):
        c = jnp.concatenate(acc, axis=1) + dwb_ref[...]
        hext_ref[0:tm, :] = jax.nn.silu(_layer_norm(c, lng_ref[...], lnb_ref[...])).astype(BF)

    def project(cb, pin):
        cols = slice(cb * MXU_COLS, (cb + 1) * MXU_COLS)
        ys.append(_dot(hext_ref[0:tm, :], wout_ref[:, cols]) + bout_ref[:, cols])

    head = [norm_rows] + [functools.partial(gated_cols, cb) for cb in range(n_blk)]
    body = [functools.partial(taps, cb, mis) for cb in range(n_blk) for mis in range(SUBLANES)]
    foot = [activate] + [functools.partial(project, cb) for cb in range(n_blk)]
    slots = [a + b + c for a, b, c in zip(_spread(head, 0, 1), _spread(body, 2, MLP_SLOTS - 6),
                                          _spread(foot, MLP_SLOTS - 5, MLP_SLOTS - 1))]
    _skewed_mlp(step, tail, o_ref, x1_scr, h_scr, slots)
    _stash(x_ref[0] + jnp.concatenate(ys, axis=1), tail, x1_scr, h_scr)


def _sgu_kernel(x_ref, gmix_ref, win_ref, bin_ref, lng_ref, lnb_ref, ws_ref, bsb_ref, wout_ref, bout_ref,
                *rest, tm, final):
    tail, (o_ref, x1_scr, h_scr) = _split_tail_refs(rest, final)
    vals = {}
    rows = []

    def gate_and_norm(pin):
        h = _rms(x_ref[0], gmix_ref[...]).astype(BF)
        z = jax.nn.gelu(_dot(h, win_ref[...]) + bin_ref[...])
        vals["u"] = z[:, :D]
        vals["v"] = _layer_norm(z[:, D:], lng_ref[...], lnb_ref[...]).astype(BF)

    def mix_chunk(c, pin):
        heads = []
        for hh in range(SGU_HEADS):
            blk = vals["v"][c * SGU_CHUNK:(c + 1) * SGU_CHUNK, hh * SGU_HEAD_DIM:(hh + 1) * SGU_HEAD_DIM]
            heads.append(_dot(ws_ref[hh], blk) + bsb_ref[hh])
        rows.append(jnp.concatenate(heads, axis=1))

    def project(pin):
        vm = jnp.concatenate(rows, axis=0)
        vals["y"] = _dot((vals["u"] * vm).astype(BF), wout_ref[...]) + bout_ref[...]

    pieces = [gate_and_norm] + [functools.partial(mix_chunk, c) for c in range(tm // SGU_CHUNK)] + [project]
    _skewed_mlp(pl.program_id(0), tail, o_ref, x1_scr, h_scr, _spread(pieces, 0, MLP_SLOTS - 1))
    _stash(x_ref[0] + vals["y"], tail, x1_scr, h_scr)


def _fourier1_kernel(x_ref, gmix_ref, g1_ref, twc_ref, tws_ref, yr_ref, yi_ref, *, bd):
    g = gmix_ref[...]
    xt = pltpu.einshape("bjd->jbd", x_ref[0])
    for j in range(SUBLANES):
        h = _rms(xt[j], g).astype(BF)
        y = _dot(g1_ref[...], h)
        yc, ys = y[:bd], y[bd:]
        tc = jnp.concatenate([twc_ref[j]] * (D // 128), axis=1)
        ts = jnp.concatenate([tws_ref[j]] * (D // 128), axis=1)
        yr_ref[0, j] = (yc * tc - ys * ts).astype(BF)
        yi_ref[0, j] = (-(yc * ts + ys * tc)).astype(BF)


def _fourier2_kernel(x_ref, yr_ref, yi_ref, p_ref, cc_ref, sc_ref, wout_ref, bout_ref, *rest, nb, ar, final):
    (gmlp_ref, wup_ref, wdn_ref, gfin_ref), (o_ref,) = _split_tail_refs(rest, final)
    qr, qi = [], []
    for bb in range(nb):
        ycat = jnp.concatenate([yr_ref[bb].reshape(ar, D), yi_ref[bb].reshape(ar, D)], axis=0)
        q = _dot(p_ref[...], ycat)
        qr.append(q[:ar])
        qi.append(q[ar:])
    qr = jnp.concatenate(qr, axis=0).astype(BF)
    qi = jnp.concatenate(qi, axis=0).astype(BF)
    fs = []
    for gi in range(D // FNET_GROUP):
        cols = slice(gi * FNET_GROUP, (gi + 1) * FNET_GROUP)
        fs.append(_dot(qr[:, cols], cc_ref[...]) + _dot(qi[:, cols], sc_ref[...]))
    f = jnp.concatenate(fs, axis=1)
    y = _dot(f.astype(BF), wout_ref[...]) + bout_ref[...]
    x1 = x_ref[...].reshape(nb * ar, D) + y
    h = _rms(x1, gmlp_ref[...]).astype(BF)
    out = _mlp(lambda: h, lambda: x1, wup_ref, wdn_ref, gfin_ref)
    o_ref[...] = out.reshape(o_ref.shape)


def _resident(shape):
    return pl.BlockSpec(shape, lambda *_: (0,) * len(shape), pipeline_mode=pl.Buffered(1))


def _params(n_axes):
    return pltpu.CompilerParams(dimension_semantics=("arbitrary",) * n_axes,
                                vmem_limit_bytes=V7X_VMEM_LIMIT)


def _row(v):
    return v.reshape(1, -1).astype(F32)


def _layer_block(shape, layer):
    return pl.BlockSpec((None,) + shape, lambda *_: (layer,) + (0,) * len(shape),
                        pipeline_mode=pl.Buffered(1))


def _tail_operands(gmlp, wup, wdn, layer, gfin):
    ops = [_row(gmlp), wup, wdn]
    specs = [_resident((1, D)), _layer_block((D, D_FF), layer), _layer_block((D_FF, D), layer)]
    if gfin is not None:
        ops.append(_row(gfin))
        specs.append(_resident((1, D)))
    return ops, specs


def _skewed_specs(tm, seq, n_tiles, halo):
    per_seq = seq // tm

    def tile(s):
        st = jnp.minimum(s, n_tiles - 1)
        return st // per_seq, st % per_seq

    def main(s):
        b, i = tile(s)
        return b, i, 0

    def out(s):
        so = jnp.maximum(s - 1, 0)
        return so // per_seq, so % per_seq, 0

    specs = [pl.BlockSpec((1, tm, D), main)]
    if halo:
        per = tm // halo
        nh = seq // halo

        def prev(s):
            b, i = tile(s)
            return b, jnp.maximum(i * per - 1, 0), 0

        def nxt(s):
            b, i = tile(s)
            return b, jnp.minimum((i + 1) * per, nh - 1), 0

        specs += [pl.BlockSpec((1, halo, D), prev), pl.BlockSpec((1, halo, D), nxt)]
    return specs, pl.BlockSpec((1, tm, D), out)


def _stash_scratch(tm):
    return [pltpu.VMEM((tm, D), F32), pltpu.VMEM((tm, D), BF)]


def _pool_layer(x, gmix, pool_w, pool_scale, gmlp, wup, wdn, layer, gfin):
    bsz, seq, _ = x.shape
    tm = min(ROW_TILE, seq)
    n_tiles = bsz * (seq // tm)
    tail_ops, tail_specs = _tail_operands(gmlp, wup, wdn, layer, gfin)
    x_specs, out_spec = _skewed_specs(tm, seq, n_tiles, POOL_HALO)
    kern = functools.partial(_pool_kernel, tm=tm, seq=seq, n_tiles=n_tiles, final=gfin is not None)
    return pl.pallas_call(
        kern,
        grid=(n_tiles + 1,),
        in_specs=x_specs + [_resident((1, D)), _resident(pool_w.shape), _resident((1, D))] + tail_specs,
        out_specs=out_spec,
        out_shape=jax.ShapeDtypeStruct(x.shape, F32),
        scratch_shapes=[pltpu.VMEM((tm + 2 * POOL_HALO, D), F32)] + _stash_scratch(tm),
        compiler_params=_params(1),
        name="pool_mlp",
    )(x, x, x, _row(gmix), pool_w.astype(BF), _row(pool_scale), *tail_ops)


def _conv_layer(x, gmix, w_in, b_in, dw, dw_b, ln_g, ln_b, w_out, b_out, gmlp, wup, wdn, layer, gfin):
    bsz, seq, _ = x.shape
    tm = min(ROW_TILE, seq)
    n_tiles = bsz * (seq // tm)
    tail_ops, tail_specs = _tail_operands(gmlp, wup, wdn, layer, gfin)
    x_specs, out_spec = _skewed_specs(tm, seq, n_tiles, CONV_HALO)
    dw_pad = jnp.zeros((CONV_WIDTH + 1, D), F32).at[:CONV_WIDTH].set(dw)
    kern = functools.partial(_conv_kernel, tm=tm, seq=seq, n_tiles=n_tiles, final=gfin is not None)
    ext = tm + 2 * CONV_HALO
    return pl.pallas_call(
        kern,
        grid=(n_tiles + 1,),
        in_specs=x_specs + [
            _resident((1, D)), _resident((D, 2 * D)), _resident((1, 2 * D)),
            _resident((CONV_WIDTH + 1, D)), _resident((1, D)), _resident((1, D)), _resident((1, D)),
            _resident((D, D)), _resident((1, D))] + tail_specs,
        out_specs=out_spec,
        out_shape=jax.ShapeDtypeStruct(x.shape, F32),
        scratch_shapes=[pltpu.VMEM((ext, D), BF), pltpu.VMEM((ext, D), F32),
                        pltpu.VMEM((2, tm + SUBLANES, D), F32)] + _stash_scratch(tm),
        compiler_params=_params(1),
        name="conv_mlp",
    )(x, x, x, _row(gmix), w_in.astype(BF), _row(b_in), dw_pad, _row(dw_b), _row(ln_g), _row(ln_b),
      w_out.astype(BF), _row(b_out), *tail_ops)


def _sgu_layer(x, gmix, w_in, b_in, ln_g, ln_b, ws, bs, w_out, b_out, gmlp, wup, wdn, layer, gfin):
    bsz, seq, _ = x.shape
    tm = min(ROW_TILE, seq)
    n_tiles = bsz * (seq // tm)
    tail_ops, tail_specs = _tail_operands(gmlp, wup, wdn, layer, gfin)
    x_specs, out_spec = _skewed_specs(tm, seq, n_tiles, 0)
    bs_b = jnp.broadcast_to(bs.astype(F32)[:, :, None], (SGU_HEADS, SGU_CHUNK, SGU_HEAD_DIM))
    kern = functools.partial(_sgu_kernel, tm=tm, final=gfin is not None)
    return pl.pallas_call(
        kern,
        grid=(n_tiles + 1,),
        in_specs=x_specs + [
            _resident((1, D)), _resident((D, 2 * D)), _resident((1, 2 * D)),
            _resident((1, D)), _resident((1, D)),
            _resident((SGU_HEADS, SGU_CHUNK, SGU_CHUNK)),
            _resident((SGU_HEADS, SGU_CHUNK, SGU_HEAD_DIM)),
            _resident((D, D)), _resident((1, D))] + tail_specs,
        out_specs=out_spec,
        out_shape=jax.ShapeDtypeStruct(x.shape, F32),
        scratch_shapes=_stash_scratch(tm),
        compiler_params=_params(1),
        name="sgu_mlp",
    )(x, _row(gmix), w_in.astype(BF), _row(b_in), _row(ln_g), _row(ln_b), ws.astype(BF), bs_b,
      w_out.astype(BF), _row(b_out), *tail_ops)


def _dft_tables(seq):
    bd = FNET_BD
    a_len = seq // bd
    r = FNET_R

    def cs(n, rows, cols):
        ang = 2.0 * np.pi * ((np.outer(rows, cols)) % n) / n
        return np.cos(ang), np.sin(ang)

    c1, s1 = cs(bd, np.arange(bd), np.arange(bd))
    g1 = np.concatenate([c1, s1], axis=0) / np.sqrt(bd)
    tc, ts = cs(seq, np.arange(a_len), np.arange(bd))
    twc = np.broadcast_to(tc[:, :, None], (a_len, bd, 128))
    tws = np.broadcast_to(ts[:, :, None], (a_len, bd, 128))
    c2, s2 = cs(a_len, np.arange(a_len), np.arange(a_len))
    eye = np.eye(r)
    kc, ks = np.kron(c2, eye), np.kron(s2, eye)
    p = np.block([[kc, ks], [-ks, kc]]) / np.sqrt(a_len)
    cc, sc = cs(FNET_GROUP, np.arange(FNET_GROUP), np.arange(FNET_GROUP))
    cc, sc = cc / np.sqrt(FNET_GROUP), sc / np.sqrt(FNET_GROUP)
    as_bf = lambda m: jnp.asarray(m, dtype=F32).astype(BF)
    return (as_bf(g1), jnp.asarray(twc, dtype=F32), jnp.asarray(tws, dtype=F32),
            as_bf(p), as_bf(cc), as_bf(sc))


def _fourier_layer(x, gmix, w_out, b_out, gmlp, wup, wdn, layer, gfin):
    bsz, seq, _ = x.shape
    bd, r = FNET_BD, FNET_R
    a_len = seq // bd
    ar = a_len * r
    nb = max(1, min(bsz, ROW_TILE // ar))
    g1, twc, tws, p, cc, sc = _dft_tables(seq)
    yshape = jax.ShapeDtypeStruct((bsz, a_len, bd, D), BF)
    yr, yi = pl.pallas_call(
        functools.partial(_fourier1_kernel, bd=bd),
        grid=(a_len // SUBLANES, bsz),
        in_specs=[pl.BlockSpec((1, bd, SUBLANES, D), lambda j, b: (b, 0, j, 0)),
                  _resident((1, D)), _resident((2 * bd, bd)),
                  pl.BlockSpec((SUBLANES, bd, 128), lambda j, b: (j, 0, 0)),
                  pl.BlockSpec((SUBLANES, bd, 128), lambda j, b: (j, 0, 0))],
        out_specs=[pl.BlockSpec((1, SUBLANES, bd, D), lambda j, b: (b, j, 0, 0)),
                   pl.BlockSpec((1, SUBLANES, bd, D), lambda j, b: (b, j, 0, 0))],
        out_shape=[yshape, yshape],
        compiler_params=_params(2),
        name="fourier_stage1",
    )(x.reshape(bsz, bd, a_len, D), _row(gmix), g1, twc, tws)

    tail_ops, tail_specs = _tail_operands(gmlp, wup, wdn, layer, gfin)
    blk = pl.BlockSpec((nb, a_len, r, D), lambda b, k: (b, 0, k, 0))
    out = pl.pallas_call(
        functools.partial(_fourier2_kernel, nb=nb, ar=ar, final=gfin is not None),
        grid=(bsz // nb, bd // r),
        in_specs=[blk, blk, blk, _resident((2 * ar, 2 * ar)),
                  _resident((FNET_GROUP, FNET_GROUP)), _resident((FNET_GROUP, FNET_GROUP)),
                  _resident((D, D)), _resident((1, D))] + tail_specs,
        out_specs=blk,
        out_shape=jax.ShapeDtypeStruct((bsz, a_len, bd, D), F32),
        compiler_params=_params(2),
        name="fourier_stage2_mlp",
    )(x.reshape(bsz, a_len, bd, D), yr, yi, p, cc, sc, w_out.astype(BF), _row(b_out), *tail_ops)
    return out.reshape(bsz, seq, D)


def _trunk(x, norm_mix_g, norm_mlp_g, w_up, w_down, final_norm_g, mixers):
    depth = norm_mix_g.shape[0]
    for i in range(depth):
        kind, j = i % 4, i // 4
        gfin = final_norm_g if i == depth - 1 else None
        common = (norm_mlp_g[i], w_up, w_down, i, gfin)
        x = mixers[kind](x, norm_mix_g[i], j, *common)
    return x


def kernel(x_prompt, x_sample, norm_mix_g, norm_mlp_g, w_up, w_down, final_norm_g, pool_w, pool_scale,
           conv_w_in, conv_b_in, conv_dw, conv_dw_b, conv_ln_g, conv_ln_b, conv_w_out, conv_b_out,
           sgu_w_in, sgu_b_in, sgu_ln_g, sgu_ln_b, sgu_ws, sgu_bs, sgu_w_out, sgu_b_out,
           fnet_w_out, fnet_b_out):
    w_up = w_up.astype(BF)
    w_down = w_down.astype(BF)

    def pool(x, g, j, *common):
        return _pool_layer(x, g, pool_w[j], pool_scale[j], *common)

    def conv(x, g, j, *common):
        return _conv_layer(x, g, conv_w_in[j], conv_b_in[j], conv_dw[j], conv_dw_b[j], conv_ln_g[j],
                           conv_ln_b[j], conv_w_out[j], conv_b_out[j], *common)

    def sgu(x, g, j, *common):
        return _sgu_layer(x, g, sgu_w_in[j], sgu_b_in[j], sgu_ln_g[j], sgu_ln_b[j], sgu_ws[j], sgu_bs[j],
                          sgu_w_out[j], sgu_b_out[j], *common)

    def fourier(x, g, j, *common):
        return _fourier_layer(x, g, fnet_w_out[j], fnet_b_out[j], *common)

    mixers = (pool, conv, sgu, fourier)
    y_prompt = _trunk(x_prompt, norm_mix_g, norm_mlp_g, w_up, w_down, final_norm_g, mixers)
    y_sample = _trunk(x_sample, norm_mix_g, norm_mlp_g, w_up, w_down, final_norm_g, mixers)
    return (y_prompt, y_sample)
```

```python
import functools

import numpy as np
import jax
import jax.numpy as jnp
from jax import lax
from jax.experimental import pallas as pl
from jax.experimental.pallas import tpu as pltpu

D = 1024
D_FF = 4 * D
EPS = 1e-6
POOL_WINDOWS = (2, 4, 8, 16)
POOL_GROUP = D // len(POOL_WINDOWS)
POOL_HALO = 8
CONV_WIDTH = 31
CONV_HALO = 16
SGU_HEADS = 4
SGU_HEAD_DIM = D // SGU_HEADS
SGU_CHUNK = 128
FNET_GROUP = 256
FNET_BD = 256
FNET_R = 16
SUBLANES = 8
MXU_COLS = 256

FF_CHUNK = 1024
ROW_TILE = 512
WIDE_ROW_TILE = 1024
V7X_VMEM_LIMIT = 60 * 1024 * 1024
MLP_SLOTS = (D_FF // FF_CHUNK) * (FF_CHUNK // MXU_COLS + D // MXU_COLS)

BF = jnp.bfloat16
F32 = jnp.float32


def _rms(x, g):
    return x * lax.rsqrt(jnp.mean(x * x, axis=-1, keepdims=True) + EPS) * g


def _layer_norm(x, g, b):
    mu = jnp.mean(x, axis=-1, keepdims=True)
    xc = x - mu
    var = jnp.mean(xc * xc, axis=-1, keepdims=True)
    return xc * lax.rsqrt(var + EPS) * g + b


def _dot(a, b):
    return jnp.dot(a, b, preferred_element_type=F32)


def _mlp(load_h, load_x1, wup_ref, wdn_ref, gfin_ref, side_work=()):
    per_chunk = FF_CHUNK // MXU_COLS
    n_out = D // MXU_COLS
    slot = [0]

    def run_side(result):
        pieces = side_work[slot[0]] if side_work else ()
        slot[0] += 1
        if pieces:
            pin = jnp.concatenate([jnp.minimum(jnp.abs(result[-1:, -128:]), 0.0)] * (D // 128), axis=1)
            for piece in pieces:
                piece(pin)

    acc = [None] * n_out
    for c in range(D_FF // FF_CHUNK):
        parts = []
        for n in range(per_chunk):
            lo = c * FF_CHUNK + n * MXU_COLS
            a = _dot(load_h(), wup_ref[:, lo:lo + MXU_COLS])
            ab = jnp.maximum(a.astype(BF), 0.0)
            parts.append(ab * ab)
            run_side(a)
        a = jnp.concatenate(parts, axis=1)
        for n in range(n_out):
            d = _dot(a, wdn_ref[c * FF_CHUNK:(c + 1) * FF_CHUNK, n * MXU_COLS:(n + 1) * MXU_COLS])
            acc[n] = d if acc[n] is None else acc[n] + d
            run_side(d)
    out = load_x1() + jnp.concatenate(acc, axis=1)
    if gfin_ref is not None:
        out = _rms(out, gfin_ref[...])
    return out


def _spread(pieces, first, last):
    slots = [[] for _ in range(MLP_SLOTS)]
    for idx, piece in enumerate(pieces):
        slots[first + idx * (last - first + 1) // len(pieces)].append(piece)
    return slots


def _split_tail_refs(refs, final):
    n = 4 if final else 3
    return tuple(refs[:n]) + (() if final else (None,)), refs[n:]


def _skewed_mlp(step, tail, o_ref, x1_scr, h_scr, side_work=()):
    @pl.when(step == 0)
    def _():
        x1_scr[...] = jnp.zeros_like(x1_scr)
        h_scr[...] = jnp.zeros_like(h_scr)

    _, wup_ref, wdn_ref, gfin_ref = tail
    o_ref[0] = _mlp(lambda: h_scr[...], lambda: x1_scr[...], wup_ref, wdn_ref, gfin_ref, side_work)


def _stash(x1, tail, x1_scr, h_scr):
    x1_scr[...] = x1
    h_scr[...] = _rms(x1, tail[0][...]).astype(BF)


def _seq_tile(step, n_tiles, tiles_per_seq):
    return jnp.minimum(step, n_tiles - 1) % tiles_per_seq


def _pool_kernel(x_ref, xp_ref, xn_ref, gmix_ref, pw_ref, psc_ref, *rest, tm, seq, n_tiles, final):
    tail, (o_ref, hext_ref, x1_scr, h_scr) = _split_tail_refs(rest, final)
    step = pl.program_id(0)
    i = _seq_tile(step, n_tiles, seq // tm)
    last = seq // tm - 1
    ys = []

    def norm_rows(pin):
        g = gmix_ref[...]
        hext_ref[0:POOL_HALO, :] = jnp.where(i > 0, _rms(xp_ref[0], g), 0.0)
        hext_ref[POOL_HALO:POOL_HALO + tm, :] = _rms(x_ref[0], g)
        hext_ref[POOL_HALO + tm:2 * POOL_HALO + tm, :] = jnp.where(i < last, _rms(xn_ref[0], g), 0.0)

    def group(gi, win, pin):
        half = win // 2
        cols = slice(gi * POOL_GROUP, (gi + 1) * POOL_GROUP)
        t = i * tm + lax.broadcasted_iota(jnp.int32, (tm, 1), 0)
        s = hext_ref[POOL_HALO - half:POOL_HALO - half + tm, cols]
        for j in range(1, win):
            s = s + hext_ref[POOL_HALO - half + j:POOL_HALO - half + j + tm, cols]
        cnt = (jnp.minimum(t + half, seq) - jnp.maximum(t - half, 0)).astype(F32)
        d = s / cnt - hext_ref[POOL_HALO:POOL_HALO + tm, cols]
        ys.append(_dot(d.astype(BF), pw_ref[gi]))

    pieces = [norm_rows] + [functools.partial(group, gi, win) for gi, win in enumerate(POOL_WINDOWS)]
    _skewed_mlp(step, tail, o_ref, x1_scr, h_scr, _spread(pieces, 0, MLP_SLOTS - 1))
    y = jnp.concatenate(ys, axis=1) * psc_ref[...]
    _stash(x_ref[0] + y, tail, x1_scr, h_scr)


def _conv_kernel(x_ref, xp_ref, xn_ref, gmix_ref, win_ref, bin_ref, dw_ref, dwb_ref, lng_ref, lnb_ref,
                 wout_ref, bout_ref, *rest, tm, seq, n_tiles, final):
    tail, (o_ref, hext_ref, gext_ref, part_ref, x1_scr, h_scr) = _split_tail_refs(rest, final)
    step = pl.program_id(0)
    i = _seq_tile(step, n_tiles, seq // tm)
    last = seq // tm - 1
    ext = tm + 2 * CONV_HALO
    base = CONV_HALO - CONV_WIDTH // 2
    n_blk = D // MXU_COLS
    acc = [None] * n_blk
    ys = []

    def norm_rows(pin):
        g = gmix_ref[...]
        hext_ref[0:CONV_HALO, :] = _rms(xp_ref[0], g).astype(BF)
        hext_ref[CONV_HALO:CONV_HALO + tm, :] = _rms(x_ref[0], g).astype(BF)
        hext_ref[CONV_HALO + tm:ext, :] = _rms(xn_ref[0], g).astype(BF)

    def gated_cols(cb, pin):
        cols = slice(cb * MXU_COLS, (cb + 1) * MXU_COLS)
        gcols = slice(D + cb * MXU_COLS, D + (cb + 1) * MXU_COLS)
        val = _dot(hext_ref[...], win_ref[:, cols]) + bin_ref[:, cols]
        gate = _dot(hext_ref[...], win_ref[:, gcols]) + bin_ref[:, gcols]
        r = lax.broadcasted_iota(jnp.int32, (ext, 1), 0)
        inside = jnp.logical_and(jnp.logical_or(r >= CONV_HALO, i > 0),
                                 jnp.logical_or(r < CONV_HALO + tm, i < last))
        gext_ref[:, cols] = jnp.where(inside, val * jax.nn.sigmoid(gate), 0.0)

    def taps(cb, mis, pin):
        cols = slice(cb * MXU_COLS, (cb + 1) * MXU_COLS)
        part = None
        for q in range((base + CONV_WIDTH - 1) // SUBLANES + 1):
            k = SUBLANES * q + mis - base
            if 0 <= k < CONV_WIDTH:
                term = (gext_ref[SUBLANES * q:SUBLANES * q + tm + SUBLANES, cols]
                        * (dw_ref[k:k + 1, cols] + pin[:, cols]))
                part = term if part is None else part + term
        if mis == 0:
            shifted = part[0:tm]
        else:
            part_ref[mis % 2, :, cols] = part
            shifted = part_ref[mis % 2, mis:mis + tm, cols]
        acc[cb] = shifted if acc[cb] is None else acc[cb] + shifted

    def activate(pin):
        c = jnp.concatenate(acc, axis=1) + dwb_ref[...]
        hext_ref[0:tm, :] = jax.nn.silu(_layer_norm(c, lng_ref[...], lnb_ref[...])).astype(BF)

    def project(cb, pin):
        cols = slice(cb * MXU_COLS, (cb + 1) * MXU_COLS)
        ys.append(_dot(hext_ref[0:tm, :], wout_ref[:, cols]) + bout_ref[:, cols])

    head = [norm_rows] + [functools.partial(gated_cols, cb) for cb in range(n_blk)]
    body = [functools.partial(taps, cb, mis) for cb in range(n_blk) for mis in range(SUBLANES)]
    foot = [activate] + [functools.partial(project, cb) for cb in range(n_blk)]
    slots = [a + b + c for a, b, c in zip(_spread(head, 0, 1), _spread(body, 2, MLP_SLOTS - 6),
                                          _spread(foot, MLP_SLOTS - 5, MLP_SLOTS - 1))]
    _skewed_mlp(step, tail, o_ref, x1_scr, h_scr, slots)
    _stash(x_ref[0] + jnp.concatenate(ys, axis=1), tail, x1_scr, h_scr)


def _sgu_kernel(x_ref, gmix_ref, win_ref, bin_ref, lng_ref, lnb_ref, ws_ref, bsb_ref, wout_ref, bout_ref,
                *rest, tm, final):
    tail, (o_ref, x1_scr, h_scr) = _split_tail_refs(rest, final)
    vals = {}
    rows = []

    def gate_and_norm(pin):
        h = _rms(x_ref[0], gmix_ref[...]).astype(BF)
        z = jax.nn.gelu(_dot(h, win_ref[...]) + bin_ref[...])
        vals["u"] = z[:, :D]
        vals["v"] = _layer_norm(z[:, D:], lng_ref[...], lnb_ref[...]).astype(BF)

    def mix_chunk(c, pin):
        heads = []
        for hh in range(SGU_HEADS):
            blk = vals["v"][c * SGU_CHUNK:(c + 1) * SGU_CHUNK, hh * SGU_HEAD_DIM:(hh + 1) * SGU_HEAD_DIM]
            heads.append(_dot(ws_ref[hh], blk) + bsb_ref[hh])
        rows.append(jnp.concatenate(heads, axis=1))

    def project(pin):
        vm = jnp.concatenate(rows, axis=0)
        vals["y"] = _dot((vals["u"] * vm).astype(BF), wout_ref[...]) + bout_ref[...]

    pieces = [gate_and_norm] + [functools.partial(mix_chunk, c) for c in range(tm // SGU_CHUNK)] + [project]
    _skewed_mlp(pl.program_id(0), tail, o_ref, x1_scr, h_scr, _spread(pieces, 0, MLP_SLOTS - 1))
    _stash(x_ref[0] + vals["y"], tail, x1_scr, h_scr)


def _fourier1_kernel(x_ref, gmix_ref, g1_ref, yr_ref, yi_ref, *, bd):
    g = gmix_ref[...]
    xt = pltpu.einshape("bjd->jbd", x_ref[0])
    for j in range(SUBLANES):
        h = _rms(xt[j], g).astype(BF)
        y = _dot(g1_ref[j], h)
        yr_ref[0, j] = y[:bd].astype(BF)
        yi_ref[0, j] = y[bd:].astype(BF)


def _fourier2_kernel(x_ref, yr_ref, yi_ref, p_ref, cc_ref, sc_ref, wout_ref, bout_ref, *rest, nb, ar, final):
    (gmlp_ref, wup_ref, wdn_ref, gfin_ref), (o_ref,) = _split_tail_refs(rest, final)
    qr, qi = [], []
    for bb in range(nb):
        ycat = jnp.concatenate([yr_ref[bb].reshape(ar, D), yi_ref[bb].reshape(ar, D)], axis=0)
        q = _dot(p_ref[...], ycat)
        qr.append(q[:ar])
        qi.append(q[ar:])
    qr = jnp.concatenate(qr, axis=0).astype(BF)
    qi = jnp.concatenate(qi, axis=0).astype(BF)
    fs = []
    for gi in range(D // FNET_GROUP):
        cols = slice(gi * FNET_GROUP, (gi + 1) * FNET_GROUP)
        fs.append(_dot(qr[:, cols], cc_ref[...]) + _dot(qi[:, cols], sc_ref[...]))
    f = jnp.concatenate(fs, axis=1)
    y = _dot(f.astype(BF), wout_ref[...]) + bout_ref[...]
    x1 = x_ref[...].reshape(nb * ar, D) + y
    h = _rms(x1, gmlp_ref[...]).astype(BF)
    out = _mlp(lambda: h, lambda: x1, wup_ref, wdn_ref, gfin_ref)
    o_ref[...] = out.reshape(o_ref.shape)


def _resident(shape):
    return pl.BlockSpec(shape, lambda *_: (0,) * len(shape), pipeline_mode=pl.Buffered(1))


def _params(n_axes):
    return pltpu.CompilerParams(dimension_semantics=("arbitrary",) * n_axes,
                                vmem_limit_bytes=V7X_VMEM_LIMIT)


def _row(v):
    return v.reshape(1, -1).astype(F32)


def _layer_block(shape, layer):
    return pl.BlockSpec((None,) + shape, lambda *_: (layer,) + (0,) * len(shape),
                        pipeline_mode=pl.Buffered(1))


def _tail_operands(gmlp, wup, wdn, layer, gfin):
    ops = [_row(gmlp), wup, wdn]
    specs = [_resident((1, D)), _layer_block((D, D_FF), layer), _layer_block((D_FF, D), layer)]
    if gfin is not None:
        ops.append(_row(gfin))
        specs.append(_resident((1, D)))
    return ops, specs


def _skewed_specs(tm, seq, n_tiles, halo):
    per_seq = seq // tm

    def tile(s):
        st = jnp.minimum(s, n_tiles - 1)
        return st // per_seq, st % per_seq

    def main(s):
        b, i = tile(s)
        return b, i, 0

    def out(s):
        so = jnp.maximum(s - 1, 0)
        return so // per_seq, so % per_seq, 0

    specs = [pl.BlockSpec((1, tm, D), main)]
    if halo:
        per = tm // halo
        nh = seq // halo

        def prev(s):
            b, i = tile(s)
            return b, jnp.maximum(i * per - 1, 0), 0

        def nxt(s):
            b, i = tile(s)
            return b, jnp.minimum((i + 1) * per, nh - 1), 0

        specs += [pl.BlockSpec((1, halo, D), prev), pl.BlockSpec((1, halo, D), nxt)]
    return specs, pl.BlockSpec((1, tm, D), out)


def _stash_scratch(tm):
    return [pltpu.VMEM((tm, D), F32), pltpu.VMEM((tm, D), BF)]


def _pool_layer(x, gmix, pool_w, pool_scale, gmlp, wup, wdn, layer, gfin):
    bsz, seq, _ = x.shape
    tm = min(WIDE_ROW_TILE, seq)
    n_tiles = bsz * (seq // tm)
    tail_ops, tail_specs = _tail_operands(gmlp, wup, wdn, layer, gfin)
    x_specs, out_spec = _skewed_specs(tm, seq, n_tiles, POOL_HALO)
    kern = functools.partial(_pool_kernel, tm=tm, seq=seq, n_tiles=n_tiles, final=gfin is not None)
    return pl.pallas_call(
        kern,
        grid=(n_tiles + 1,),
        in_specs=x_specs + [_resident((1, D)), _resident(pool_w.shape), _resident((1, D))] + tail_specs,
        out_specs=out_spec,
        out_shape=jax.ShapeDtypeStruct(x.shape, F32),
        scratch_shapes=[pltpu.VMEM((tm + 2 * POOL_HALO, D), F32)] + _stash_scratch(tm),
        compiler_params=_params(1),
        name="pool_mlp",
    )(x, x, x, _row(gmix), pool_w.astype(BF), _row(pool_scale), *tail_ops)


def _conv_layer(x, gmix, w_in, b_in, dw, dw_b, ln_g, ln_b, w_out, b_out, gmlp, wup, wdn, layer, gfin):
    bsz, seq, _ = x.shape
    tm = min(ROW_TILE, seq)
    n_tiles = bsz * (seq // tm)
    tail_ops, tail_specs = _tail_operands(gmlp, wup, wdn, layer, gfin)
    x_specs, out_spec = _skewed_specs(tm, seq, n_tiles, CONV_HALO)
    dw_pad = jnp.zeros((CONV_WIDTH + 1, D), F32).at[:CONV_WIDTH].set(dw)
    kern = functools.partial(_conv_kernel, tm=tm, seq=seq, n_tiles=n_tiles, final=gfin is not None)
    ext = tm + 2 * CONV_HALO
    return pl.pallas_call(
        kern,
        grid=(n_tiles + 1,),
        in_specs=x_specs + [
            _resident((1, D)), _resident((D, 2 * D)), _resident((1, 2 * D)),
            _resident((CONV_WIDTH + 1, D)), _resident((1, D)), _resident((1, D)), _resident((1, D)),
            _resident((D, D)), _resident((1, D))] + tail_specs,
        out_specs=out_spec,
        out_shape=jax.ShapeDtypeStruct(x.shape, F32),
        scratch_shapes=[pltpu.VMEM((ext, D), BF), pltpu.VMEM((ext, D), F32),
                        pltpu.VMEM((2, tm + SUBLANES, D), F32)] + _stash_scratch(tm),
        compiler_params=_params(1),
        name="conv_mlp",
    )(x, x, x, _row(gmix), w_in.astype(BF), _row(b_in), dw_pad, _row(dw_b), _row(ln_g), _row(ln_b),
      w_out.astype(BF), _row(b_out), *tail_ops)


def _sgu_layer(x, gmix, w_in, b_in, ln_g, ln_b, ws, bs, w_out, b_out, gmlp, wup, wdn, layer, gfin):
    bsz, seq, _ = x.shape
    tm = min(ROW_TILE, seq)
    n_tiles = bsz * (seq // tm)
    tail_ops, tail_specs = _tail_operands(gmlp, wup, wdn, layer, gfin)
    x_specs, out_spec = _skewed_specs(tm, seq, n_tiles, 0)
    bs_b = jnp.broadcast_to(bs.astype(F32)[:, :, None], (SGU_HEADS, SGU_CHUNK, SGU_HEAD_DIM))
    kern = functools.partial(_sgu_kernel, tm=tm, final=gfin is not None)
    return pl.pallas_call(
        kern,
        grid=(n_tiles + 1,),
        in_specs=x_specs + [
            _resident((1, D)), _resident((D, 2 * D)), _resident((1, 2 * D)),
            _resident((1, D)), _resident((1, D)),
            _resident((SGU_HEADS, SGU_CHUNK, SGU_CHUNK)),
            _resident((SGU_HEADS, SGU_CHUNK, SGU_HEAD_DIM)),
            _resident((D, D)), _resident((1, D))] + tail_specs,
        out_specs=out_spec,
        out_shape=jax.ShapeDtypeStruct(x.shape, F32),
        scratch_shapes=_stash_scratch(tm),
        compiler_params=_params(1),
        name="sgu_mlp",
    )(x, _row(gmix), w_in.astype(BF), _row(b_in), _row(ln_g), _row(ln_b), ws.astype(BF), bs_b,
      w_out.astype(BF), _row(b_out), *tail_ops)


def _dft_tables(seq):
    bd = FNET_BD
    a_len = seq // bd
    r = FNET_R

    def cs(n, rows, cols):
        ang = 2.0 * np.pi * ((np.outer(rows, cols)) % n) / n
        return np.cos(ang), np.sin(ang)

    n = a_len * np.arange(bd)[None, None, :] + np.arange(a_len)[:, None, None]
    ang = 2.0 * np.pi * ((np.arange(bd)[None, :, None] * n) % seq) / seq
    g1 = np.concatenate([np.cos(ang), -np.sin(ang)], axis=1) / np.sqrt(bd)
    c2, s2 = cs(a_len, np.arange(a_len), np.arange(a_len))
    eye = np.eye(r)
    kc, ks = np.kron(c2, eye), np.kron(s2, eye)
    p = np.block([[kc, ks], [-ks, kc]]) / np.sqrt(a_len)
    cc, sc = cs(FNET_GROUP, np.arange(FNET_GROUP), np.arange(FNET_GROUP))
    cc, sc = cc / np.sqrt(FNET_GROUP), sc / np.sqrt(FNET_GROUP)
    as_bf = lambda m: jnp.asarray(m, dtype=F32).astype(BF)
    return as_bf(g1), as_bf(p), as_bf(cc), as_bf(sc)


def _fourier_layer(x, gmix, w_out, b_out, gmlp, wup, wdn, layer, gfin):
    bsz, seq, _ = x.shape
    bd, r = FNET_BD, FNET_R
    a_len = seq // bd
    ar = a_len * r
    nb = max(1, min(bsz, ROW_TILE // ar))
    g1, p, cc, sc = _dft_tables(seq)
    yshape = jax.ShapeDtypeStruct((bsz, a_len, bd, D), BF)
    yr, yi = pl.pallas_call(
        functools.partial(_fourier1_kernel, bd=bd),
        grid=(a_len // SUBLANES, bsz),
        in_specs=[pl.BlockSpec((1, bd, SUBLANES, D), lambda j, b: (b, 0, j, 0)),
                  _resident((1, D)),
                  pl.BlockSpec((SUBLANES, 2 * bd, bd), lambda j, b: (j, 0, 0))],
        out_specs=[pl.BlockSpec((1, SUBLANES, bd, D), lambda j, b: (b, j, 0, 0)),
                   pl.BlockSpec((1, SUBLANES, bd, D), lambda j, b: (b, j, 0, 0))],
        out_shape=[yshape, yshape],
        compiler_params=_params(2),
        name="fourier_stage1",
    )(x.reshape(bsz, bd, a_len, D), _row(gmix), g1)

    tail_ops, tail_specs = _tail_operands(gmlp, wup, wdn, layer, gfin)
    blk = pl.BlockSpec((nb, a_len, r, D), lambda b, k: (b, 0, k, 0))
    out = pl.pallas_call(
        functools.partial(_fourier2_kernel, nb=nb, ar=ar, final=gfin is not None),
        grid=(bsz // nb, bd // r),
        in_specs=[blk, blk, blk, _resident((2 * ar, 2 * ar)),
                  _resident((FNET_GROUP, FNET_GROUP)), _resident((FNET_GROUP, FNET_GROUP)),
                  _resident((D, D)), _resident((1, D))] + tail_specs,
        out_specs=blk,
        out_shape=jax.ShapeDtypeStruct((bsz, a_len, bd, D), F32),
        compiler_params=_params(2),
        name="fourier_stage2_mlp",
    )(x.reshape(bsz, a_len, bd, D), yr, yi, p, cc, sc, w_out.astype(BF), _row(b_out), *tail_ops)
    return out.reshape(bsz, seq, D)


def _trunk(x, norm_mix_g, norm_mlp_g, w_up, w_down, final_norm_g, mixers):
    depth = norm_mix_g.shape[0]
    for i in range(depth):
        kind, j = i % 4, i // 4
        gfin = final_norm_g if i == depth - 1 else None
        common = (norm_mlp_g[i], w_up, w_down, i, gfin)
        x = mixers[kind](x, norm_mix_g[i], j, *common)
    return x


def kernel(x_prompt, x_sample, norm_mix_g, norm_mlp_g, w_up, w_down, final_norm_g, pool_w, pool_scale,
           conv_w_in, conv_b_in, conv_dw, conv_dw_b, conv_ln_g, conv_ln_b, conv_w_out, conv_b_out,
           sgu_w_in, sgu_b_in, sgu_ln_g, sgu_ln_b, sgu_ws, sgu_bs, sgu_w_out, sgu_b_out,
           fnet_w_out, fnet_b_out):
    w_up = w_up.astype(BF)
    w_down = w_down.astype(BF)

    def pool(x, g, j, *common):
        return _pool_layer(x, g, pool_w[j], pool_scale[j], *common)

    def conv(x, g, j, *common):
        return _conv_layer(x, g, conv_w_in[j], conv_b_in[j], conv_dw[j], conv_dw_b[j], conv_ln_g[j],
                           conv_ln_b[j], conv_w_out[j], conv_b_out[j], *common)

    def sgu(x, g, j, *common):
        return _sgu_layer(x, g, sgu_w_in[j], sgu_b_in[j], sgu_ln_g[j], sgu_ln_b[j], sgu_ws[j], sgu_bs[j],
                          sgu_w_out[j], sgu_b_out[j], *common)

    def fourier(x, g, j, *common):
        return _fourier_layer(x, g, fnet_w_out[j], fnet_b_out[j], *common)

    mixers = (pool, conv, sgu, fourier)
    y_prompt = _trunk(x_prompt, norm_mix_g, norm_mlp_g, w_up, w_down, final_norm_g, mixers)
    y_sample = _trunk(x_sample, norm_mix_g, norm_mlp_g, w_up, w_down, final_norm_g, mixers)
    return (y_prompt, y_sample)
```

```python
import functools

import numpy as np
import jax
import jax.numpy as jnp
from jax import lax
from jax.experimental import pallas as pl
from jax.experimental.pallas import tpu as pltpu

D = 1024
D_FF = 4 * D
EPS = 1e-6
POOL_WINDOWS = (2, 4, 8, 16)
POOL_GROUP = D // len(POOL_WINDOWS)
POOL_HALO = 8
CONV_WIDTH = 31
CONV_HALO = 16
SGU_HEADS = 4
SGU_HEAD_DIM = D // SGU_HEADS
SGU_CHUNK = 128
FNET_GROUP = 256
FNET_BD = 256
FNET_R = 16
SUBLANES = 8
MXU_COLS = 256

FF_CHUNK = 2048
ROW_TILE = 512
V7X_VMEM_LIMIT = 56 * 1024 * 1024
MLP_SLOTS = (D_FF // FF_CHUNK) * (FF_CHUNK // MXU_COLS + D // MXU_COLS)

BF = jnp.bfloat16
F32 = jnp.float32


def _rms(x, g):
    return x * lax.rsqrt(jnp.mean(x * x, axis=-1, keepdims=True) + EPS) * g


def _layer_norm(x, g, b):
    mu = jnp.mean(x, axis=-1, keepdims=True)
    xc = x - mu
    var = jnp.mean(xc * xc, axis=-1, keepdims=True)
    return xc * lax.rsqrt(var + EPS) * g + b


def _dot(a, b):
    return jnp.dot(a, b, preferred_element_type=F32)


def _mlp(load_h, load_x1, wup_ref, wdn_ref, gfin_ref, side_work=()):
    per_chunk = FF_CHUNK // MXU_COLS
    n_out = D // MXU_COLS
    slot = [0]

    def run_side(result):
        pieces = side_work[slot[0]] if side_work else ()
        slot[0] += 1
        if pieces:
            pin = jnp.concatenate([jnp.minimum(jnp.abs(result[-1:, -128:]), 0.0)] * (D // 128), axis=1)
            for piece in pieces:
                piece(pin)

    acc = [None] * n_out
    for c in range(D_FF // FF_CHUNK):
        parts = []
        for n in range(per_chunk):
            lo = c * FF_CHUNK + n * MXU_COLS
            a = _dot(load_h(), wup_ref[:, lo:lo + MXU_COLS])
            ab = jnp.maximum(a.astype(BF), 0.0)
            parts.append(ab * ab)
            run_side(a)
        a = jnp.concatenate(parts, axis=1)
        for n in range(n_out):
            d = _dot(a, wdn_ref[c * FF_CHUNK:(c + 1) * FF_CHUNK, n * MXU_COLS:(n + 1) * MXU_COLS])
            acc[n] = d if acc[n] is None else acc[n] + d
            run_side(d)
    out = load_x1() + jnp.concatenate(acc, axis=1)
    if gfin_ref is not None:
        out = _rms(out, gfin_ref[...])
    return out


def _spread(pieces, first, last):
    slots = [[] for _ in range(MLP_SLOTS)]
    for idx, piece in enumerate(pieces):
        slots[first + idx * (last - first + 1) // len(pieces)].append(piece)
    return slots


def _split_tail_refs(refs, final):
    n = 4 if final else 3
    return tuple(refs[:n]) + (() if final else (None,)), refs[n:]


def _skewed_mlp(step, tail, o_ref, x1_scr, h_scr, side_work=()):
    @pl.when(step == 0)
    def _():
        x1_scr[...] = jnp.zeros_like(x1_scr)
        h_scr[...] = jnp.zeros_like(h_scr)

    _, wup_ref, wdn_ref, gfin_ref = tail
    o_ref[0] = _mlp(lambda: h_scr[...], lambda: x1_scr[...], wup_ref, wdn_ref, gfin_ref, side_work)


def _stash(x1, tail, x1_scr, h_scr):
    x1_scr[...] = x1
    h_scr[...] = _rms(x1, tail[0][...]).astype(BF)


def _seq_tile(step, n_tiles, tiles_per_seq):
    return jnp.minimum(step, n_tiles - 1) % tiles_per_seq


def _pool_kernel(x_ref, xp_ref, xn_ref, gmix_ref, pw_ref, psc_ref, *rest, tm, seq, n_tiles, final):
    tail, (o_ref, hext_ref, x1_scr, h_scr) = _split_tail_refs(rest, final)
    step = pl.program_id(0)
    i = _seq_tile(step, n_tiles, seq // tm)
    last = seq // tm - 1
    ys = []

    def norm_rows(pin):
        g = gmix_ref[...]
        hext_ref[0:POOL_HALO, :] = jnp.where(i > 0, _rms(xp_ref[0], g), 0.0)
        hext_ref[POOL_HALO:POOL_HALO + tm, :] = _rms(x_ref[0], g)
        hext_ref[POOL_HALO + tm:2 * POOL_HALO + tm, :] = jnp.where(i < last, _rms(xn_ref[0], g), 0.0)

    def group(gi, win, pin):
        half = win // 2
        cols = slice(gi * POOL_GROUP, (gi + 1) * POOL_GROUP)
        t = i * tm + lax.broadcasted_iota(jnp.int32, (tm, 1), 0)
        s = hext_ref[POOL_HALO - half:POOL_HALO - half + tm, cols]
        for j in range(1, win):
            s = s + hext_ref[POOL_HALO - half + j:POOL_HALO - half + j + tm, cols]
        cnt = (jnp.minimum(t + half, seq) - jnp.maximum(t - half, 0)).astype(F32)
        d = s / cnt - hext_ref[POOL_HALO:POOL_HALO + tm, cols]
        ys.append(_dot(d.astype(BF), pw_ref[gi]))

    pieces = [norm_rows] + [functools.partial(group, gi, win) for gi, win in enumerate(POOL_WINDOWS)]
    _skewed_mlp(step, tail, o_ref, x1_scr, h_scr, _spread(pieces, 0, MLP_SLOTS - 1))
    y = jnp.concatenate(ys, axis=1) * psc_ref[...]
    _stash(x_ref[0] + y, tail, x1_scr, h_scr)


def _conv_kernel(x_ref, xp_ref, xn_ref, gmix_ref, win_ref, bin_ref, dw_ref, dwb_ref, lng_ref, lnb_ref,
                 wout_ref, bout_ref, *rest, tm, seq, n_tiles, final):
    tail, (o_ref, hext_ref, gext_ref, part_ref, x1_scr, h_scr) = _split_tail_refs(rest, final)
    step = pl.program_id(0)
    i = _seq_tile(step, n_tiles, seq // tm)
    last = seq // tm - 1
    ext = tm + 2 * CONV_HALO
    base = CONV_HALO - CONV_WIDTH // 2
    n_blk = D // MXU_COLS
    acc = [None] * n_blk
    ys = []

    def norm_rows(pin):
        g = gmix_ref[...]
        hext_ref[0:CONV_HALO, :] = _rms(xp_ref[0], g).astype(BF)
        hext_ref[CONV_HALO:CONV_HALO + tm, :] = _rms(x_ref[0], g).astype(BF)
        hext_ref[CONV_HALO + tm:ext, :] = _rms(xn_ref[0], g).astype(BF)

    def gated_cols(cb, pin):
        cols = slice(cb * MXU_COLS, (cb + 1) * MXU_COLS)
        gcols = slice(D + cb * MXU_COLS, D + (cb + 1) * MXU_COLS)
        val = _dot(hext_ref[...], win_ref[:, cols]) + bin_ref[:, cols]
        gate = _dot(hext_ref[...], win_ref[:, gcols]) + bin_ref[:, gcols]
        r = lax.broadcasted_iota(jnp.int32, (ext, 1), 0)
        inside = jnp.logical_and(jnp.logical_or(r >= CONV_HALO, i > 0),
                                 jnp.logical_or(r < CONV_HALO + tm, i < last))
        gext_ref[:, cols] = jnp.where(inside, val * jax.nn.sigmoid(gate), 0.0)

    def taps(cb, mis, pin):
        cols = slice(cb * MXU_COLS, (cb + 1) * MXU_COLS)
        part = None
        for q in range((base + CONV_WIDTH - 1) // SUBLANES + 1):
            k = SUBLANES * q + mis - base
            if 0 <= k < CONV_WIDTH:
                term = (gext_ref[SUBLANES * q:SUBLANES * q + tm + SUBLANES, cols]
                        * (dw_ref[k:k + 1, cols] + pin[:, cols]))
                part = term if part is None else part + term
        if mis == 0:
            shifted = part[0:tm]
        else:
            part_ref[mis % 2, :, cols] = part
            shifted = part_ref[mis % 2, mis:mis + tm, cols]
        acc[cb] = shifted if acc[cb] is None else acc[cb] + shifted

    def activate(pin):
        c = jnp.concatenate(acc, axis=1) + dwb_ref[...]
        hext_ref[0:tm, :] = jax.nn.silu(_layer_norm(c, lng_ref[...], lnb_ref[...])).astype(BF)

    def project(cb, pin):
        cols = slice(cb * MXU_COLS, (cb + 1) * MXU_COLS)
        ys.append(_dot(hext_ref[0:tm, :], wout_ref[:, cols]) + bout_ref[:, cols])

    head = [norm_rows] + [functools.partial(gated_cols, cb) for cb in range(n_blk)]
    body = [functools.partial(taps, cb, mis) for cb in range(n_blk) for mis in range(SUBLANES)]
    foot = [activate] + [functools.partial(project, cb) for cb in range(n_blk)]
    slots = [a + b + c for a, b, c in zip(_spread(head, 0, 0), _spread(body, 1, MLP_SLOTS - 8),
                                          _spread(foot, MLP_SLOTS - 7, MLP_SLOTS - 1))]
    _skewed_mlp(step, tail, o_ref, x1_scr, h_scr, slots)
    _stash(x_ref[0] + jnp.concatenate(ys, axis=1), tail, x1_scr, h_scr)


def _sgu_kernel(x_ref, gmix_ref, win_ref, bin_ref, lng_ref, lnb_ref, ws_ref, bsb_ref, wout_ref, bout_ref,
                *rest, tm, final):
    tail, (o_ref, x1_scr, h_scr) = _split_tail_refs(rest, final)
    vals = {}
    rows = []

    def gate_and_norm(pin):
        h = _rms(x_ref[0], gmix_ref[...]).astype(BF)
        z = jax.nn.gelu(_dot(h, win_ref[...]) + bin_ref[...])
        vals["u"] = z[:, :D]
        vals["v"] = _layer_norm(z[:, D:], lng_ref[...], lnb_ref[...]).astype(BF)

    def mix_chunk(c, pin):
        heads = []
        for hh in range(SGU_HEADS):
            blk = vals["v"][c * SGU_CHUNK:(c + 1) * SGU_CHUNK, hh * SGU_HEAD_DIM:(hh + 1) * SGU_HEAD_DIM]
            heads.append(_dot(ws_ref[hh], blk) + bsb_ref[hh])
        rows.append(jnp.concatenate(heads, axis=1))

    def project(pin):
        vm = jnp.concatenate(rows, axis=0)
        vals["y"] = _dot((vals["u"] * vm).astype(BF), wout_ref[...]) + bout_ref[...]

    pieces = [gate_and_norm] + [functools.partial(mix_chunk, c) for c in range(tm // SGU_CHUNK)] + [project]
    _skewed_mlp(pl.program_id(0), tail, o_ref, x1_scr, h_scr, _spread(pieces, 0, MLP_SLOTS - 1))
    _stash(x_ref[0] + vals["y"], tail, x1_scr, h_scr)


def _fourier1_kernel(x_ref, gmix_ref, g1_ref, yr_ref, yi_ref, *, bd):
    g = gmix_ref[...]
    xt = pltpu.einshape("bjd->jbd", x_ref[0])
    for j in range(SUBLANES):
        h = _rms(xt[j], g).astype(BF)
        y = _dot(g1_ref[j], h)
        yr_ref[0, j] = y[:bd].astype(BF)
        yi_ref[0, j] = y[bd:].astype(BF)


def _fourier2_kernel(x_ref, yr_ref, yi_ref, p_ref, cc_ref, sc_ref, wout_ref, bout_ref, *rest, nb, ar, final):
    (gmlp_ref, wup_ref, wdn_ref, gfin_ref), (o_ref,) = _split_tail_refs(rest, final)
    qr, qi = [], []
    for bb in range(nb):
        ycat = jnp.concatenate([yr_ref[bb].reshape(ar, D), yi_ref[bb].reshape(ar, D)], axis=0)
        q = _dot(p_ref[...], ycat)
        qr.append(q[:ar])
        qi.append(q[ar:])
    qr = jnp.concatenate(qr, axis=0).astype(BF)
    qi = jnp.concatenate(qi, axis=0).astype(BF)
    fs = []
    for gi in range(D // FNET_GROUP):
        cols = slice(gi * FNET_GROUP, (gi + 1) * FNET_GROUP)
        fs.append(_dot(qr[:, cols], cc_ref[...]) + _dot(qi[:, cols], sc_ref[...]))
    f = jnp.concatenate(fs, axis=1)
    y = _dot(f.astype(BF), wout_ref[...]) + bout_ref[...]
    x1 = x_ref[...].reshape(nb * ar, D) + y
    h = _rms(x1, gmlp_ref[...]).astype(BF)
    out = _mlp(lambda: h, lambda: x1, wup_ref, wdn_ref, gfin_ref)
    o_ref[...] = out.reshape(o_ref.shape)


def _resident(shape):
    return pl.BlockSpec(shape, lambda *_: (0,) * len(shape), pipeline_mode=pl.Buffered(1))


def _params(n_axes):
    return pltpu.CompilerParams(dimension_semantics=("arbitrary",) * n_axes,
                                vmem_limit_bytes=V7X_VMEM_LIMIT)


def _row(v):
    return v.reshape(1, -1).astype(F32)


def _layer_block(shape, layer):
    return pl.BlockSpec((None,) + shape, lambda *_: (layer,) + (0,) * len(shape),
                        pipeline_mode=pl.Buffered(1))


def _tail_operands(gmlp, wup, wdn, layer, gfin):
    ops = [_row(gmlp), wup, wdn]
    specs = [_resident((1, D)), _layer_block((D, D_FF), layer), _layer_block((D_FF, D), layer)]
    if gfin is not None:
        ops.append(_row(gfin))
        specs.append(_resident((1, D)))
    return ops, specs


def _skewed_specs(tm, seq, n_tiles, halo):
    per_seq = seq // tm

    def tile(s):
        st = jnp.minimum(s, n_tiles - 1)
        return st // per_seq, st % per_seq

    def main(s):
        b, i = tile(s)
        return b, i, 0

    def out(s):
        so = jnp.maximum(s - 1, 0)
        return so // per_seq, so % per_seq, 0

    specs = [pl.BlockSpec((1, tm, D), main)]
    if halo:
        per = tm // halo
        nh = seq // halo

        def prev(s):
            b, i = tile(s)
            return b, jnp.maximum(i * per - 1, 0), 0

        def nxt(s):
            b, i = tile(s)
            return b, jnp.minimum((i + 1) * per, nh - 1), 0

        specs += [pl.BlockSpec((1, halo, D), prev), pl.BlockSpec((1, halo, D), nxt)]
    return specs, pl.BlockSpec((1, tm, D), out)


def _stash_scratch(tm):
    return [pltpu.VMEM((tm, D), F32), pltpu.VMEM((tm, D), BF)]


def _pool_layer(x, gmix, pool_w, pool_scale, gmlp, wup, wdn, layer, gfin):
    bsz, seq, _ = x.shape
    tm = min(ROW_TILE, seq)
    n_tiles = bsz * (seq // tm)
    tail_ops, tail_specs = _tail_operands(gmlp, wup, wdn, layer, gfin)
    x_specs, out_spec = _skewed_specs(tm, seq, n_tiles, POOL_HALO)
    kern = functools.partial(_pool_kernel, tm=tm, seq=seq, n_tiles=n_tiles, final=gfin is not None)
    return pl.pallas_call(
        kern,
        grid=(n_tiles + 1,),
        in_specs=x_specs + [_resident((1, D)), _resident(pool_w.shape), _resident((1, D))] + tail_specs,
        out_specs=out_spec,
        out_shape=jax.ShapeDtypeStruct(x.shape, F32),
        scratch_shapes=[pltpu.VMEM((tm + 2 * POOL_HALO, D), F32)] + _stash_scratch(tm),
        compiler_params=_params(1),
        name="pool_mlp",
    )(x, x, x, _row(gmix), pool_w.astype(BF), _row(pool_scale), *tail_ops)


def _conv_layer(x, gmix, w_in, b_in, dw, dw_b, ln_g, ln_b, w_out, b_out, gmlp, wup, wdn, layer, gfin):
    bsz, seq, _ = x.shape
    tm = min(ROW_TILE, seq)
    n_tiles = bsz * (seq // tm)
    tail_ops, tail_specs = _tail_operands(gmlp, wup, wdn, layer, gfin)
    x_specs, out_spec = _skewed_specs(tm, seq, n_tiles, CONV_HALO)
    dw_pad = jnp.zeros((CONV_WIDTH + 1, D), F32).at[:CONV_WIDTH].set(dw)
    kern = functools.partial(_conv_kernel, tm=tm, seq=seq, n_tiles=n_tiles, final=gfin is not None)
    ext = tm + 2 * CONV_HALO
    return pl.pallas_call(
        kern,
        grid=(n_tiles + 1,),
        in_specs=x_specs + [
            _resident((1, D)), _resident((D, 2 * D)), _resident((1, 2 * D)),
            _resident((CONV_WIDTH + 1, D)), _resident((1, D)), _resident((1, D)), _resident((1, D)),
            _resident((D, D)), _resident((1, D))] + tail_specs,
        out_specs=out_spec,
        out_shape=jax.ShapeDtypeStruct(x.shape, F32),
        scratch_shapes=[pltpu.VMEM((ext, D), BF), pltpu.VMEM((ext, D), F32),
                        pltpu.VMEM((2, tm + SUBLANES, D), F32)] + _stash_scratch(tm),
        compiler_params=_params(1),
        name="conv_mlp",
    )(x, x, x, _row(gmix), w_in.astype(BF), _row(b_in), dw_pad, _row(dw_b), _row(ln_g), _row(ln_b),
      w_out.astype(BF), _row(b_out), *tail_ops)


def _sgu_layer(x, gmix, w_in, b_in, ln_g, ln_b, ws, bs, w_out, b_out, gmlp, wup, wdn, layer, gfin):
    bsz, seq, _ = x.shape
    tm = min(ROW_TILE, seq)
    n_tiles = bsz * (seq // tm)
    tail_ops, tail_specs = _tail_operands(gmlp, wup, wdn, layer, gfin)
    x_specs, out_spec = _skewed_specs(tm, seq, n_tiles, 0)
    bs_b = jnp.broadcast_to(bs.astype(F32)[:, :, None], (SGU_HEADS, SGU_CHUNK, SGU_HEAD_DIM))
    kern = functools.partial(_sgu_kernel, tm=tm, final=gfin is not None)
    return pl.pallas_call(
        kern,
        grid=(n_tiles + 1,),
        in_specs=x_specs + [
            _resident((1, D)), _resident((D, 2 * D)), _resident((1, 2 * D)),
            _resident((1, D)), _resident((1, D)),
            _resident((SGU_HEADS, SGU_CHUNK, SGU_CHUNK)),
            _resident((SGU_HEADS, SGU_CHUNK, SGU_HEAD_DIM)),
            _resident((D, D)), _resident((1, D))] + tail_specs,
        out_specs=out_spec,
        out_shape=jax.ShapeDtypeStruct(x.shape, F32),
        scratch_shapes=_stash_scratch(tm),
        compiler_params=_params(1),
        name="sgu_mlp",
    )(x, _row(gmix), w_in.astype(BF), _row(b_in), _row(ln_g), _row(ln_b), ws.astype(BF), bs_b,
      w_out.astype(BF), _row(b_out), *tail_ops)


def _dft_tables(seq):
    bd = FNET_BD
    a_len = seq // bd
    r = FNET_R

    def cs(n, rows, cols):
        ang = 2.0 * np.pi * ((np.outer(rows, cols)) % n) / n
        return np.cos(ang), np.sin(ang)

    n = a_len * np.arange(bd)[None, None, :] + np.arange(a_len)[:, None, None]
    ang = 2.0 * np.pi * ((np.arange(bd)[None, :, None] * n) % seq) / seq
    g1 = np.concatenate([np.cos(ang), -np.sin(ang)], axis=1) / np.sqrt(bd)
    c2, s2 = cs(a_len, np.arange(a_len), np.arange(a_len))
    eye = np.eye(r)
    kc, ks = np.kron(c2, eye), np.kron(s2, eye)
    p = np.block([[kc, ks], [-ks, kc]]) / np.sqrt(a_len)
    cc, sc = cs(FNET_GROUP, np.arange(FNET_GROUP), np.arange(FNET_GROUP))
    cc, sc = cc / np.sqrt(FNET_GROUP), sc / np.sqrt(FNET_GROUP)
    as_bf = lambda m: jnp.asarray(m, dtype=F32).astype(BF)
    return as_bf(g1), as_bf(p), as_bf(cc), as_bf(sc)


def _fourier_layer(x, gmix, w_out, b_out, gmlp, wup, wdn, layer, gfin):
    bsz, seq, _ = x.shape
    bd, r = FNET_BD, FNET_R
    a_len = seq // bd
    ar = a_len * r
    nb = max(1, min(bsz, ROW_TILE // ar))
    g1, p, cc, sc = _dft_tables(seq)
    yshape = jax.ShapeDtypeStruct((bsz, a_len, bd, D), BF)
    yr, yi = pl.pallas_call(
        functools.partial(_fourier1_kernel, bd=bd),
        grid=(a_len // SUBLANES, bsz),
        in_specs=[pl.BlockSpec((1, bd, SUBLANES, D), lambda j, b: (b, 0, j, 0)),
                  _resident((1, D)),
                  pl.BlockSpec((SUBLANES, 2 * bd, bd), lambda j, b: (j, 0, 0))],
        out_specs=[pl.BlockSpec((1, SUBLANES, bd, D), lambda j, b: (b, j, 0, 0)),
                   pl.BlockSpec((1, SUBLANES, bd, D), lambda j, b: (b, j, 0, 0))],
        out_shape=[yshape, yshape],
        compiler_params=_params(2),
        name="fourier_stage1",
    )(x.reshape(bsz, bd, a_len, D), _row(gmix), g1)

    tail_ops, tail_specs = _tail_operands(gmlp, wup, wdn, layer, gfin)
    blk = pl.BlockSpec((nb, a_len, r, D), lambda b, k: (b, 0, k, 0))
    out = pl.pallas_call(
        functools.partial(_fourier2_kernel, nb=nb, ar=ar, final=gfin is not None),
        grid=(bsz // nb, bd // r),
        in_specs=[blk, blk, blk, _resident((2 * ar, 2 * ar)),
                  _resident((FNET_GROUP, FNET_GROUP)), _resident((FNET_GROUP, FNET_GROUP)),
                  _resident((D, D)), _resident((1, D))] + tail_specs,
        out_specs=blk,
        out_shape=jax.ShapeDtypeStruct((bsz, a_len, bd, D), F32),
        compiler_params=_params(2),
        name="fourier_stage2_mlp",
    )(x.reshape(bsz, a_len, bd, D), yr, yi, p, cc, sc, w_out.astype(BF), _row(b_out), *tail_ops)
    return out.reshape(bsz, seq, D)


def _trunk(x, norm_mix_g, norm_mlp_g, w_up, w_down, final_norm_g, mixers):
    depth = norm_mix_g.shape[0]
    for i in range(depth):
        kind, j = i % 4, i // 4
        gfin = final_norm_g if i == depth - 1 else None
        common = (norm_mlp_g[i], w_up, w_down, i, gfin)
        x = mixers[kind](x, norm_mix_g[i], j, *common)
    return x


def kernel(x_prompt, x_sample, norm_mix_g, norm_mlp_g, w_up, w_down, final_norm_g, pool_w, pool_scale,
           conv_w_in, conv_b_in, conv_dw, conv_dw_b, conv_ln_g, conv_ln_b, conv_w_out, conv_b_out,
           sgu_w_in, sgu_b_in, sgu_ln_g, sgu_ln_b, sgu_ws, sgu_bs, sgu_w_out, sgu_b_out,
           fnet_w_out, fnet_b_out):
    w_up = w_up.astype(BF)
    w_down = w_down.astype(BF)

    def pool(x, g, j, *common):
        return _pool_layer(x, g, pool_w[j], pool_scale[j], *common)

    def conv(x, g, j, *common):
        return _conv_layer(x, g, conv_w_in[j], conv_b_in[j], conv_dw[j], conv_dw_b[j], conv_ln_g[j],
                           conv_ln_b[j], conv_w_out[j], conv_b_out[j], *common)

    def sgu(x, g, j, *common):
        return _sgu_layer(x, g, sgu_w_in[j], sgu_b_in[j], sgu_ln_g[j], sgu_ln_b[j], sgu_ws[j], sgu_bs[j],
                          sgu_w_out[j], sgu_b_out[j], *common)

    def fourier(x, g, j, *common):
        return _fourier_layer(x, g, fnet_w_out[j], fnet_b_out[j], *common)

    mixers = (pool, conv, sgu, fourier)
    y_prompt = _trunk(x_prompt, norm_mix_g, norm_mlp_g, w_up, w_down, final_norm_g, mixers)
    y_sample = _trunk(x_sample, norm_mix_g, norm_mlp_g, w_up, w_down, final_norm_g, mixers)
    return (y_prompt, y_sample)
```

```python
import functools

import numpy as np
import jax
import jax.numpy as jnp
from jax import lax
from jax.experimental import pallas as pl
from jax.experimental.pallas import tpu as pltpu

D = 1024
D_FF = 4 * D
EPS = 1e-6
POOL_WINDOWS = (2, 4, 8, 16)
POOL_GROUP = D // len(POOL_WINDOWS)
POOL_HALO = 8
CONV_WIDTH = 31
CONV_HALO = 16
SGU_HEADS = 4
SGU_HEAD_DIM = D // SGU_HEADS
SGU_CHUNK = 128
FNET_GROUP = 256
FNET_BD = 256
FNET_R = 16
SUBLANES = 8
MXU_COLS = 256

FF_CHUNK = 1024
WIDE_FF_CHUNK = 2048
ROW_TILE = 512
V7X_VMEM_LIMIT = 56 * 1024 * 1024


def _mlp_slots(ff_chunk):
    return (D_FF // ff_chunk) * (ff_chunk // MXU_COLS + D // MXU_COLS)


MLP_SLOTS = _mlp_slots(FF_CHUNK)

BF = jnp.bfloat16
F32 = jnp.float32


def _rms(x, g):
    return x * lax.rsqrt(jnp.mean(x * x, axis=-1, keepdims=True) + EPS) * g


def _layer_norm(x, g, b):
    mu = jnp.mean(x, axis=-1, keepdims=True)
    xc = x - mu
    var = jnp.mean(xc * xc, axis=-1, keepdims=True)
    return xc * lax.rsqrt(var + EPS) * g + b


def _dot(a, b):
    return jnp.dot(a, b, preferred_element_type=F32)


def _mlp(load_h, load_x1, wup_ref, wdn_ref, gfin_ref, side_work=(), ff_chunk=FF_CHUNK):
    per_chunk = ff_chunk // MXU_COLS
    n_out = D // MXU_COLS
    slot = [0]

    def run_side(result):
        pieces = side_work[slot[0]] if side_work else ()
        slot[0] += 1
        if pieces:
            pin = jnp.concatenate([jnp.minimum(jnp.abs(result[-1:, -128:]), 0.0)] * (D // 128), axis=1)
            for piece in pieces:
                piece(pin)

    acc = [None] * n_out
    for c in range(D_FF // ff_chunk):
        parts = []
        for n in range(per_chunk):
            lo = c * ff_chunk + n * MXU_COLS
            a = _dot(load_h(), wup_ref[:, lo:lo + MXU_COLS])
            ab = jnp.maximum(a.astype(BF), 0.0)
            parts.append(ab * ab)
            run_side(a)
        a = jnp.concatenate(parts, axis=1)
        for n in range(n_out):
            d = _dot(a, wdn_ref[c * ff_chunk:(c + 1) * ff_chunk, n * MXU_COLS:(n + 1) * MXU_COLS])
            acc[n] = d if acc[n] is None else acc[n] + d
            run_side(d)
    out = load_x1() + jnp.concatenate(acc, axis=1)
    if gfin_ref is not None:
        out = _rms(out, gfin_ref[...])
    return out


def _spread(pieces, first, last, n_slots=MLP_SLOTS):
    slots = [[] for _ in range(n_slots)]
    for idx, piece in enumerate(pieces):
        slots[first + idx * (last - first + 1) // len(pieces)].append(piece)
    return slots


def _split_tail_refs(refs, final):
    n = 4 if final else 3
    return tuple(refs[:n]) + (() if final else (None,)), refs[n:]


def _skewed_mlp(step, tail, o_ref, x1_scr, h_scr, side_work=(), ff_chunk=FF_CHUNK):
    @pl.when(step == 0)
    def _():
        x1_scr[...] = jnp.zeros_like(x1_scr)
        h_scr[...] = jnp.zeros_like(h_scr)

    _, wup_ref, wdn_ref, gfin_ref = tail
    o_ref[0] = _mlp(lambda: h_scr[...], lambda: x1_scr[...], wup_ref, wdn_ref, gfin_ref, side_work, ff_chunk)


def _stash(x1, tail, x1_scr, h_scr):
    x1_scr[...] = x1
    h_scr[...] = _rms(x1, tail[0][...]).astype(BF)


def _seq_tile(step, n_tiles, tiles_per_seq):
    return jnp.minimum(step, n_tiles - 1) % tiles_per_seq


def _pool_kernel(x_ref, xp_ref, xn_ref, gmix_ref, pw_ref, psc_ref, *rest, tm, seq, n_tiles, final):
    tail, (o_ref, hext_ref, x1_scr, h_scr) = _split_tail_refs(rest, final)
    step = pl.program_id(0)
    i = _seq_tile(step, n_tiles, seq // tm)
    last = seq // tm - 1
    ys = []

    def norm_rows(pin):
        g = gmix_ref[...]
        hext_ref[0:POOL_HALO, :] = jnp.where(i > 0, _rms(xp_ref[0], g), 0.0)
        hext_ref[POOL_HALO:POOL_HALO + tm, :] = _rms(x_ref[0], g)
        hext_ref[POOL_HALO + tm:2 * POOL_HALO + tm, :] = jnp.where(i < last, _rms(xn_ref[0], g), 0.0)

    def group(gi, win, pin):
        half = win // 2
        cols = slice(gi * POOL_GROUP, (gi + 1) * POOL_GROUP)
        t = i * tm + lax.broadcasted_iota(jnp.int32, (tm, 1), 0)
        s = hext_ref[POOL_HALO - half:POOL_HALO - half + tm, cols]
        for j in range(1, win):
            s = s + hext_ref[POOL_HALO - half + j:POOL_HALO - half + j + tm, cols]
        cnt = (jnp.minimum(t + half, seq) - jnp.maximum(t - half, 0)).astype(F32)
        d = s / cnt - hext_ref[POOL_HALO:POOL_HALO + tm, cols]
        ys.append(_dot(d.astype(BF), pw_ref[gi]))

    pieces = [norm_rows] + [functools.partial(group, gi, win) for gi, win in enumerate(POOL_WINDOWS)]
    n_slots = _mlp_slots(WIDE_FF_CHUNK)
    _skewed_mlp(step, tail, o_ref, x1_scr, h_scr, _spread(pieces, 0, n_slots - 1, n_slots), WIDE_FF_CHUNK)
    y = jnp.concatenate(ys, axis=1) * psc_ref[...]
    _stash(x_ref[0] + y, tail, x1_scr, h_scr)


def _conv_kernel(x_ref, xp_ref, xn_ref, gmix_ref, win_ref, bin_ref, dw_ref, dwb_ref, lng_ref, lnb_ref,
                 wout_ref, bout_ref, *rest, tm, seq, n_tiles, final):
    tail, (o_ref, hext_ref, gext_ref, part_ref, x1_scr, h_scr) = _split_tail_refs(rest, final)
    step = pl.program_id(0)
    i = _seq_tile(step, n_tiles, seq // tm)
    last = seq // tm - 1
    ext = tm + 2 * CONV_HALO
    base = CONV_HALO - CONV_WIDTH // 2
    n_blk = D // MXU_COLS
    acc = [None] * n_blk
    ys = []

    def norm_rows(pin):
        g = gmix_ref[...]
        hext_ref[0:CONV_HALO, :] = _rms(xp_ref[0], g).astype(BF)
        hext_ref[CONV_HALO:CONV_HALO + tm, :] = _rms(x_ref[0], g).astype(BF)
        hext_ref[CONV_HALO + tm:ext, :] = _rms(xn_ref[0], g).astype(BF)

    def gated_cols(cb, pin):
        cols = slice(cb * MXU_COLS, (cb + 1) * MXU_COLS)
        gcols = slice(D + cb * MXU_COLS, D + (cb + 1) * MXU_COLS)
        val = _dot(hext_ref[...], win_ref[:, cols]) + bin_ref[:, cols]
        gate = _dot(hext_ref[...], win_ref[:, gcols]) + bin_ref[:, gcols]
        r = lax.broadcasted_iota(jnp.int32, (ext, 1), 0)
        inside = jnp.logical_and(jnp.logical_or(r >= CONV_HALO, i > 0),
                                 jnp.logical_or(r < CONV_HALO + tm, i < last))
        gext_ref[:, cols] = jnp.where(inside, val * jax.nn.sigmoid(gate), 0.0)

    def taps(cb, mis, pin):
        cols = slice(cb * MXU_COLS, (cb + 1) * MXU_COLS)
        part = None
        for q in range((base + CONV_WIDTH - 1) // SUBLANES + 1):
            k = SUBLANES * q + mis - base
            if 0 <= k < CONV_WIDTH:
                term = (gext_ref[SUBLANES * q:SUBLANES * q + tm + SUBLANES, cols]
                        * (dw_ref[k:k + 1, cols] + pin[:, cols]))
                part = term if part is None else part + term
        if mis == 0:
            shifted = part[0:tm]
        else:
            part_ref[mis % 2, :, cols] = part
            shifted = part_ref[mis % 2, mis:mis + tm, cols]
        acc[cb] = shifted if acc[cb] is None else acc[cb] + shifted

    def activate(pin):
        c = jnp.concatenate(acc, axis=1) + dwb_ref[...]
        hext_ref[0:tm, :] = jax.nn.silu(_layer_norm(c, lng_ref[...], lnb_ref[...])).astype(BF)

    def project(cb, pin):
        cols = slice(cb * MXU_COLS, (cb + 1) * MXU_COLS)
        ys.append(_dot(hext_ref[0:tm, :], wout_ref[:, cols]) + bout_ref[:, cols])

    head = [norm_rows] + [functools.partial(gated_cols, cb) for cb in range(n_blk)]
    body = [functools.partial(taps, cb, mis) for cb in range(n_blk) for mis in range(SUBLANES)]
    foot = [activate] + [functools.partial(project, cb) for cb in range(n_blk)]
    slots = [a + b + c for a, b, c in zip(_spread(head, 0, 1), _spread(body, 2, MLP_SLOTS - 6),
                                          _spread(foot, MLP_SLOTS - 5, MLP_SLOTS - 1))]
    _skewed_mlp(step, tail, o_ref, x1_scr, h_scr, slots)
    _stash(x_ref[0] + jnp.concatenate(ys, axis=1), tail, x1_scr, h_scr)


def _sgu_kernel(x_ref, gmix_ref, win_ref, bin_ref, lng_ref, lnb_ref, ws_ref, bsb_ref, wout_ref, bout_ref,
                *rest, tm, final):
    tail, (o_ref, x1_scr, h_scr) = _split_tail_refs(rest, final)
    vals = {}
    rows = []

    def gate_and_norm(pin):
        h = _rms(x_ref[0], gmix_ref[...]).astype(BF)
        z = jax.nn.gelu(_dot(h, win_ref[...]) + bin_ref[...])
        vals["u"] = z[:, :D]
        vals["v"] = _layer_norm(z[:, D:], lng_ref[...], lnb_ref[...]).astype(BF)

    def mix_chunk(c, pin):
        heads = []
        for hh in range(SGU_HEADS):
            blk = vals["v"][c * SGU_CHUNK:(c + 1) * SGU_CHUNK, hh * SGU_HEAD_DIM:(hh + 1) * SGU_HEAD_DIM]
            heads.append(_dot(ws_ref[hh], blk) + bsb_ref[hh])
        rows.append(jnp.concatenate(heads, axis=1))

    def project(pin):
        vm = jnp.concatenate(rows, axis=0)
        vals["y"] = _dot((vals["u"] * vm).astype(BF), wout_ref[...]) + bout_ref[...]

    pieces = [gate_and_norm] + [functools.partial(mix_chunk, c) for c in range(tm // SGU_CHUNK)] + [project]
    _skewed_mlp(pl.program_id(0), tail, o_ref, x1_scr, h_scr, _spread(pieces, 0, MLP_SLOTS - 1))
    _stash(x_ref[0] + vals["y"], tail, x1_scr, h_scr)


def _fourier1_kernel(x_ref, gmix_ref, g1_ref, yr_ref, yi_ref, *, bd):
    g = gmix_ref[...]
    xt = pltpu.einshape("bjd->jbd", x_ref[0])
    for j in range(SUBLANES):
        h = _rms(xt[j], g).astype(BF)
        y = _dot(g1_ref[j], h)
        yr_ref[0, j] = y[:bd].astype(BF)
        yi_ref[0, j] = y[bd:].astype(BF)


def _fourier2_kernel(x_ref, yr_ref, yi_ref, p_ref, cc_ref, sc_ref, wout_ref, bout_ref, *rest, nb, ar, final):
    (gmlp_ref, wup_ref, wdn_ref, gfin_ref), (o_ref,) = _split_tail_refs(rest, final)
    qr, qi = [], []
    for bb in range(nb):
        ycat = jnp.concatenate([yr_ref[bb].reshape(ar, D), yi_ref[bb].reshape(ar, D)], axis=0)
        q = _dot(p_ref[...], ycat)
        qr.append(q[:ar])
        qi.append(q[ar:])
    qr = jnp.concatenate(qr, axis=0).astype(BF)
    qi = jnp.concatenate(qi, axis=0).astype(BF)
    fs = []
    for gi in range(D // FNET_GROUP):
        cols = slice(gi * FNET_GROUP, (gi + 1) * FNET_GROUP)
        fs.append(_dot(qr[:, cols], cc_ref[...]) + _dot(qi[:, cols], sc_ref[...]))
    f = jnp.concatenate(fs, axis=1)
    y = _dot(f.astype(BF), wout_ref[...]) + bout_ref[...]
    x1 = x_ref[...].reshape(nb * ar, D) + y
    h = _rms(x1, gmlp_ref[...]).astype(BF)
    out = _mlp(lambda: h, lambda: x1, wup_ref, wdn_ref, gfin_ref, ff_chunk=WIDE_FF_CHUNK)
    o_ref[...] = out.reshape(o_ref.shape)


def _resident(shape):
    return pl.BlockSpec(shape, lambda *_: (0,) * len(shape), pipeline_mode=pl.Buffered(1))


def _params(n_axes):
    return pltpu.CompilerParams(dimension_semantics=("arbitrary",) * n_axes,
                                vmem_limit_bytes=V7X_VMEM_LIMIT)


def _row(v):
    return v.reshape(1, -1).astype(F32)


def _layer_block(shape, layer):
    return pl.BlockSpec((None,) + shape, lambda *_: (layer,) + (0,) * len(shape),
                        pipeline_mode=pl.Buffered(1))


def _tail_operands(gmlp, wup, wdn, layer, gfin):
    ops = [_row(gmlp), wup, wdn]
    specs = [_resident((1, D)), _layer_block((D, D_FF), layer), _layer_block((D_FF, D), layer)]
    if gfin is not None:
        ops.append(_row(gfin))
        specs.append(_resident((1, D)))
    return ops, specs


def _skewed_specs(tm, seq, n_tiles, halo):
    per_seq = seq // tm

    def tile(s):
        st = jnp.minimum(s, n_tiles - 1)
        return st // per_seq, st % per_seq

    def main(s):
        b, i = tile(s)
        return b, i, 0

    def out(s):
        so = jnp.maximum(s - 1, 0)
        return so // per_seq, so % per_seq, 0

    specs = [pl.BlockSpec((1, tm, D), main)]
    if halo:
        per = tm // halo
        nh = seq // halo

        def prev(s):
            b, i = tile(s)
            return b, jnp.maximum(i * per - 1, 0), 0

        def nxt(s):
            b, i = tile(s)
            return b, jnp.minimum((i + 1) * per, nh - 1), 0

        specs += [pl.BlockSpec((1, halo, D), prev), pl.BlockSpec((1, halo, D), nxt)]
    return specs, pl.BlockSpec((1, tm, D), out)


def _stash_scratch(tm):
    return [pltpu.VMEM((tm, D), F32), pltpu.VMEM((tm, D), BF)]


def _pool_layer(x, gmix, pool_w, pool_scale, gmlp, wup, wdn, layer, gfin):
    bsz, seq, _ = x.shape
    tm = min(ROW_TILE, seq)
    n_tiles = bsz * (seq // tm)
    tail_ops, tail_specs = _tail_operands(gmlp, wup, wdn, layer, gfin)
    x_specs, out_spec = _skewed_specs(tm, seq, n_tiles, POOL_HALO)
    kern = functools.partial(_pool_kernel, tm=tm, seq=seq, n_tiles=n_tiles, final=gfin is not None)
    return pl.pallas_call(
        kern,
        grid=(n_tiles + 1,),
        in_specs=x_specs + [_resident((1, D)), _resident(pool_w.shape), _resident((1, D))] + tail_specs,
        out_specs=out_spec,
        out_shape=jax.ShapeDtypeStruct(x.shape, F32),
        scratch_shapes=[pltpu.VMEM((tm + 2 * POOL_HALO, D), F32)] + _stash_scratch(tm),
        compiler_params=_params(1),
        name="pool_mlp",
    )(x, x, x, _row(gmix), pool_w.astype(BF), _row(pool_scale), *tail_ops)


def _conv_layer(x, gmix, w_in, b_in, dw, dw_b, ln_g, ln_b, w_out, b_out, gmlp, wup, wdn, layer, gfin):
    bsz, seq, _ = x.shape
    tm = min(ROW_TILE, seq)
    n_tiles = bsz * (seq // tm)
    tail_ops, tail_specs = _tail_operands(gmlp, wup, wdn, layer, gfin)
    x_specs, out_spec = _skewed_specs(tm, seq, n_tiles, CONV_HALO)
    dw_pad = jnp.zeros((CONV_WIDTH + 1, D), F32).at[:CONV_WIDTH].set(dw)
    kern = functools.partial(_conv_kernel, tm=tm, seq=seq, n_tiles=n_tiles, final=gfin is not None)
    ext = tm + 2 * CONV_HALO
    return pl.pallas_call(
        kern,
        grid=(n_tiles + 1,),
        in_specs=x_specs + [
            _resident((1, D)), _resident((D, 2 * D)), _resident((1, 2 * D)),
            _resident((CONV_WIDTH + 1, D)), _resident((1, D)), _resident((1, D)), _resident((1, D)),
            _resident((D, D)), _resident((1, D))] + tail_specs,
        out_specs=out_spec,
        out_shape=jax.ShapeDtypeStruct(x.shape, F32),
        scratch_shapes=[pltpu.VMEM((ext, D), BF), pltpu.VMEM((ext, D), F32),
                        pltpu.VMEM((2, tm + SUBLANES, D), F32)] + _stash_scratch(tm),
        compiler_params=_params(1),
        name="conv_mlp",
    )(x, x, x, _row(gmix), w_in.astype(BF), _row(b_in), dw_pad, _row(dw_b), _row(ln_g), _row(ln_b),
      w_out.astype(BF), _row(b_out), *tail_ops)


def _sgu_layer(x, gmix, w_in, b_in, ln_g, ln_b, ws, bs, w_out, b_out, gmlp, wup, wdn, layer, gfin):
    bsz, seq, _ = x.shape
    tm = min(ROW_TILE, seq)
    n_tiles = bsz * (seq // tm)
    tail_ops, tail_specs = _tail_operands(gmlp, wup, wdn, layer, gfin)
    x_specs, out_spec = _skewed_specs(tm, seq, n_tiles, 0)
    bs_b = jnp.broadcast_to(bs.astype(F32)[:, :, None], (SGU_HEADS, SGU_CHUNK, SGU_HEAD_DIM))
    kern = functools.partial(_sgu_kernel, tm=tm, final=gfin is not None)
    return pl.pallas_call(
        kern,
        grid=(n_tiles + 1,),
        in_specs=x_specs + [
            _resident((1, D)), _resident((D, 2 * D)), _resident((1, 2 * D)),
            _resident((1, D)), _resident((1, D)),
            _resident((SGU_HEADS, SGU_CHUNK, SGU_CHUNK)),
            _resident((SGU_HEADS, SGU_CHUNK, SGU_HEAD_DIM)),
            _resident((D, D)), _resident((1, D))] + tail_specs,
        out_specs=out_spec,
        out_shape=jax.ShapeDtypeStruct(x.shape, F32),
        scratch_shapes=_stash_scratch(tm),
        compiler_params=_params(1),
        name="sgu_mlp",
    )(x, _row(gmix), w_in.astype(BF), _row(b_in), _row(ln_g), _row(ln_b), ws.astype(BF), bs_b,
      w_out.astype(BF), _row(b_out), *tail_ops)


def _dft_tables(seq):
    bd = FNET_BD
    a_len = seq // bd
    r = FNET_R

    def cs(n, rows, cols):
        ang = 2.0 * np.pi * ((np.outer(rows, cols)) % n) / n
        return np.cos(ang), np.sin(ang)

    n = a_len * np.arange(bd)[None, None, :] + np.arange(a_len)[:, None, None]
    ang = 2.0 * np.pi * ((np.arange(bd)[None, :, None] * n) % seq) / seq
    g1 = np.concatenate([np.cos(ang), -np.sin(ang)], axis=1) / np.sqrt(bd)
    c2, s2 = cs(a_len, np.arange(a_len), np.arange(a_len))
    eye = np.eye(r)
    kc, ks = np.kron(c2, eye), np.kron(s2, eye)
    p = np.block([[kc, ks], [-ks, kc]]) / np.sqrt(a_len)
    cc, sc = cs(FNET_GROUP, np.arange(FNET_GROUP), np.arange(FNET_GROUP))
    cc, sc = cc / np.sqrt(FNET_GROUP), sc / np.sqrt(FNET_GROUP)
    as_bf = lambda m: jnp.asarray(m, dtype=F32).astype(BF)
    return as_bf(g1), as_bf(p), as_bf(cc), as_bf(sc)


def _fourier_layer(x, gmix, w_out, b_out, gmlp, wup, wdn, layer, gfin):
    bsz, seq, _ = x.shape
    bd, r = FNET_BD, FNET_R
    a_len = seq // bd
    ar = a_len * r
    nb = max(1, min(bsz, ROW_TILE // ar))
    g1, p, cc, sc = _dft_tables(seq)
    yshape = jax.ShapeDtypeStruct((bsz, a_len, bd, D), BF)
    yr, yi = pl.pallas_call(
        functools.partial(_fourier1_kernel, bd=bd),
        grid=(a_len // SUBLANES, bsz),
        in_specs=[pl.BlockSpec((1, bd, SUBLANES, D), lambda j, b: (b, 0, j, 0)),
                  _resident((1, D)),
                  pl.BlockSpec((SUBLANES, 2 * bd, bd), lambda j, b: (j, 0, 0))],
        out_specs=[pl.BlockSpec((1, SUBLANES, bd, D), lambda j, b: (b, j, 0, 0)),
                   pl.BlockSpec((1, SUBLANES, bd, D), lambda j, b: (b, j, 0, 0))],
        out_shape=[yshape, yshape],
        compiler_params=_params(2),
        name="fourier_stage1",
    )(x.reshape(bsz, bd, a_len, D), _row(gmix), g1)

    tail_ops, tail_specs = _tail_operands(gmlp, wup, wdn, layer, gfin)
    blk = pl.BlockSpec((nb, a_len, r, D), lambda b, k: (b, 0, k, 0))
    out = pl.pallas_call(
        functools.partial(_fourier2_kernel, nb=nb, ar=ar, final=gfin is not None),
        grid=(bsz // nb, bd // r),
        in_specs=[blk, blk, blk, _resident((2 * ar, 2 * ar)),
                  _resident((FNET_GROUP, FNET_GROUP)), _resident((FNET_GROUP, FNET_GROUP)),
                  _resident((D, D)), _resident((1, D))] + tail_specs,
        out_specs=blk,
        out_shape=jax.ShapeDtypeStruct((bsz, a_len, bd, D), F32),
        compiler_params=_params(2),
        name="fourier_stage2_mlp",
    )(x.reshape(bsz, a_len, bd, D), yr, yi, p, cc, sc, w_out.astype(BF), _row(b_out), *tail_ops)
    return out.reshape(bsz, seq, D)


def _trunk(x, norm_mix_g, norm_mlp_g, w_up, w_down, final_norm_g, mixers):
    depth = norm_mix_g.shape[0]
    for i in range(depth):
        kind, j = i % 4, i // 4
        gfin = final_norm_g if i == depth - 1 else None
        common = (norm_mlp_g[i], w_up, w_down, i, gfin)
        x = mixers[kind](x, norm_mix_g[i], j, *common)
    return x


def kernel(x_prompt, x_sample, norm_mix_g, norm_mlp_g, w_up, w_down, final_norm_g, pool_w, pool_scale,
           conv_w_in, conv_b_in, conv_dw, conv_dw_b, conv_ln_g, conv_ln_b, conv_w_out, conv_b_out,
           sgu_w_in, sgu_b_in, sgu_ln_g, sgu_ln_b, sgu_ws, sgu_bs, sgu_w_out, sgu_b_out,
           fnet_w_out, fnet_b_out):
    w_up = w_up.astype(BF)
    w_down = w_down.astype(BF)

    def pool(x, g, j, *common):
        return _pool_layer(x, g, pool_w[j], pool_scale[j], *common)

    def conv(x, g, j, *common):
        return _conv_layer(x, g, conv_w_in[j], conv_b_in[j], conv_dw[j], conv_dw_b[j], conv_ln_g[j],
                           conv_ln_b[j], conv_w_out[j], conv_b_out[j], *common)

    def sgu(x, g, j, *common):
        return _sgu_layer(x, g, sgu_w_in[j], sgu_b_in[j], sgu_ln_g[j], sgu_ln_b[j], sgu_ws[j], sgu_bs[j],
                          sgu_w_out[j], sgu_b_out[j], *common)

    def fourier(x, g, j, *common):
        return _fourier_layer(x, g, fnet_w_out[j], fnet_b_out[j], *common)

    mixers = (pool, conv, sgu, fourier)
    y_prompt = _trunk(x_prompt, norm_mix_g, norm_mlp_g, w_up, w_down, final_norm_g, mixers)
    y_sample = _trunk(x_sample, norm_mix_g, norm_mlp_g, w_up, w_down, final_norm_g, mixers)
    return (y_prompt, y_sample)
```

```python
import functools

import numpy as np
import jax
import jax.numpy as jnp
from jax import lax
from jax.experimental import pallas as pl
from jax.experimental.pallas import tpu as pltpu

D = 1024
D_FF = 4 * D
EPS = 1e-6
POOL_WINDOWS = (2, 4, 8, 16)
POOL_GROUP = D // len(POOL_WINDOWS)
POOL_HALO = 8
CONV_WIDTH = 31
CONV_HALO = 16
SGU_HEADS = 4
SGU_HEAD_DIM = D // SGU_HEADS
SGU_CHUNK = 128
FNET_GROUP = 256
FNET_BD = 256
FNET_R = 16
SUBLANES = 8
MXU_COLS = 256

FF_CHUNK = 1024
WIDE_FF_CHUNK = 2048
ROW_TILE = 512
V7X_VMEM_LIMIT = 56 * 1024 * 1024


def _mlp_slots(ff_chunk):
    return (D_FF // ff_chunk) * (ff_chunk // MXU_COLS + D // MXU_COLS)


MLP_SLOTS = _mlp_slots(FF_CHUNK)

BF = jnp.bfloat16
F32 = jnp.float32


def _rms(x, g):
    return x * lax.rsqrt(jnp.mean(x * x, axis=-1, keepdims=True) + EPS) * g


def _layer_norm(x, g, b):
    mu = jnp.mean(x, axis=-1, keepdims=True)
    xc = x - mu
    var = jnp.mean(xc * xc, axis=-1, keepdims=True)
    return xc * lax.rsqrt(var + EPS) * g + b


def _dot(a, b):
    return jnp.dot(a, b, preferred_element_type=F32)


def _mlp(load_h, load_x1, wup_ref, wdn_ref, gfin_ref, side_work=(), ff_chunk=FF_CHUNK):
    per_chunk = ff_chunk // MXU_COLS
    n_out = D // MXU_COLS
    slot = [0]

    def run_side(result):
        pieces = side_work[slot[0]] if side_work else ()
        slot[0] += 1
        if pieces:
            pin = jnp.concatenate([jnp.minimum(jnp.abs(result[-1:, -128:]), 0.0)] * (D // 128), axis=1)
            for piece in pieces:
                piece(pin)

    acc = [None] * n_out
    for c in range(D_FF // ff_chunk):
        parts = []
        for n in range(per_chunk):
            lo = c * ff_chunk + n * MXU_COLS
            a = _dot(load_h(), wup_ref[:, lo:lo + MXU_COLS])
            ab = jnp.maximum(a.astype(BF), 0.0)
            parts.append(ab * ab)
            run_side(a)
        a = jnp.concatenate(parts, axis=1)
        for n in range(n_out):
            d = _dot(a, wdn_ref[c * ff_chunk:(c + 1) * ff_chunk, n * MXU_COLS:(n + 1) * MXU_COLS])
            acc[n] = d if acc[n] is None else acc[n] + d
            run_side(d)
    out = load_x1() + jnp.concatenate(acc, axis=1)
    if gfin_ref is not None:
        out = _rms(out, gfin_ref[...])
    return out


def _spread(pieces, first, last, n_slots=MLP_SLOTS):
    slots = [[] for _ in range(n_slots)]
    for idx, piece in enumerate(pieces):
        slots[first + idx * (last - first + 1) // len(pieces)].append(piece)
    return slots


def _split_tail_refs(refs, final):
    n = 4 if final else 3
    return tuple(refs[:n]) + (() if final else (None,)), refs[n:]


def _skewed_mlp(step, tail, o_ref, x1_scr, h_scr, side_work=(), ff_chunk=FF_CHUNK):
    @pl.when(step == 0)
    def _():
        x1_scr[...] = jnp.zeros_like(x1_scr)
        h_scr[...] = jnp.zeros_like(h_scr)

    _, wup_ref, wdn_ref, gfin_ref = tail
    o_ref[0] = _mlp(lambda: h_scr[...], lambda: x1_scr[...], wup_ref, wdn_ref, gfin_ref, side_work, ff_chunk)


def _stash(x1, tail, x1_scr, h_scr):
    x1_scr[...] = x1
    h_scr[...] = _rms(x1, tail[0][...]).astype(BF)


def _seq_tile(step, n_tiles, tiles_per_seq):
    return jnp.minimum(step, n_tiles - 1) % tiles_per_seq


def _pool_kernel(x_ref, xp_ref, xn_ref, gmix_ref, pw_ref, psc_ref, *rest, tm, seq, n_tiles, final):
    tail, (o_ref, hext_ref, x1_scr, h_scr) = _split_tail_refs(rest, final)
    step = pl.program_id(0)
    i = _seq_tile(step, n_tiles, seq // tm)
    last = seq // tm - 1
    ys = []

    def norm_rows(pin):
        g = gmix_ref[...]
        hext_ref[0:POOL_HALO, :] = jnp.where(i > 0, _rms(xp_ref[0], g), 0.0)
        hext_ref[POOL_HALO:POOL_HALO + tm, :] = _rms(x_ref[0], g)
        hext_ref[POOL_HALO + tm:2 * POOL_HALO + tm, :] = jnp.where(i < last, _rms(xn_ref[0], g), 0.0)

    def group(gi, win, pin):
        half = win // 2
        cols = slice(gi * POOL_GROUP, (gi + 1) * POOL_GROUP)
        t = i * tm + lax.broadcasted_iota(jnp.int32, (tm, 1), 0)
        s = hext_ref[POOL_HALO - half:POOL_HALO - half + tm, cols]
        for j in range(1, win):
            s = s + hext_ref[POOL_HALO - half + j:POOL_HALO - half + j + tm, cols]
        cnt = (jnp.minimum(t + half, seq) - jnp.maximum(t - half, 0)).astype(F32)
        d = s / cnt - hext_ref[POOL_HALO:POOL_HALO + tm, cols]
        ys.append(_dot(d.astype(BF), pw_ref[gi]))

    pieces = [norm_rows] + [functools.partial(group, gi, win) for gi, win in enumerate(POOL_WINDOWS)]
    n_slots = _mlp_slots(WIDE_FF_CHUNK)
    _skewed_mlp(step, tail, o_ref, x1_scr, h_scr, _spread(pieces, 0, n_slots - 1, n_slots), WIDE_FF_CHUNK)
    y = jnp.concatenate(ys, axis=1) * psc_ref[...]
    _stash(x_ref[0] + y, tail, x1_scr, h_scr)


def _conv_kernel(x_ref, xp_ref, xn_ref, gmix_ref, win_ref, bin_ref, dw_ref, dwb_ref, lng_ref, lnb_ref,
                 wout_ref, bout_ref, *rest, tm, seq, n_tiles, final):
    tail, (o_ref, hext_ref, gext_ref, part_ref, x1_scr, h_scr) = _split_tail_refs(rest, final)
    step = pl.program_id(0)
    i = _seq_tile(step, n_tiles, seq // tm)
    last = seq // tm - 1
    ext = tm + 2 * CONV_HALO
    base = CONV_HALO - CONV_WIDTH // 2
    n_blk = D // MXU_COLS
    acc = [None] * n_blk
    ys = []

    def norm_rows(pin):
        g = gmix_ref[...]
        hext_ref[0:CONV_HALO, :] = _rms(xp_ref[0], g).astype(BF)
        hext_ref[CONV_HALO:CONV_HALO + tm, :] = _rms(x_ref[0], g).astype(BF)
        hext_ref[CONV_HALO + tm:ext, :] = _rms(xn_ref[0], g).astype(BF)

    def gated_cols(cb, pin):
        cols = slice(cb * MXU_COLS, (cb + 1) * MXU_COLS)
        gcols = slice(D + cb * MXU_COLS, D + (cb + 1) * MXU_COLS)
        val = _dot(hext_ref[...], win_ref[:, cols]) + bin_ref[:, cols]
        gate = _dot(hext_ref[...], win_ref[:, gcols]) + bin_ref[:, gcols]
        r = lax.broadcasted_iota(jnp.int32, (ext, 1), 0)
        inside = jnp.logical_and(jnp.logical_or(r >= CONV_HALO, i > 0),
                                 jnp.logical_or(r < CONV_HALO + tm, i < last))
        gext_ref[:, cols] = jnp.where(inside, val * jax.nn.sigmoid(gate), 0.0)

    def taps(cb, mis, pin):
        cols = slice(cb * MXU_COLS, (cb + 1) * MXU_COLS)
        part = None
        for q in range((base + CONV_WIDTH - 1) // SUBLANES + 1):
            k = SUBLANES * q + mis - base
            if 0 <= k < CONV_WIDTH:
                term = (gext_ref[SUBLANES * q:SUBLANES * q + tm + SUBLANES, cols]
                        * (dw_ref[k:k + 1, cols] + pin[:, cols]))
                part = term if part is None else part + term
        if mis == 0:
            shifted = part[0:tm]
        else:
            part_ref[mis % 2, :, cols] = part
            shifted = part_ref[mis % 2, mis:mis + tm, cols]
        acc[cb] = shifted if acc[cb] is None else acc[cb] + shifted

    def activate(pin):
        c = jnp.concatenate(acc, axis=1) + dwb_ref[...]
        hext_ref[0:tm, :] = jax.nn.silu(_layer_norm(c, lng_ref[...], lnb_ref[...])).astype(BF)

    def project(cb, pin):
        cols = slice(cb * MXU_COLS, (cb + 1) * MXU_COLS)
        ys.append(_dot(hext_ref[0:tm, :], wout_ref[:, cols]) + bout_ref[:, cols])

    head = [norm_rows] + [functools.partial(gated_cols, cb) for cb in range(n_blk)]
    body = [functools.partial(taps, cb, mis) for cb in range(n_blk) for mis in range(SUBLANES)]
    foot = [activate] + [functools.partial(project, cb) for cb in range(n_blk)]
    slots = [a + b + c for a, b, c in zip(_spread(head, 0, 0), _spread(body, 1, MLP_SLOTS - 8),
                                          _spread(foot, MLP_SLOTS - 7, MLP_SLOTS - 1))]
    _skewed_mlp(step, tail, o_ref, x1_scr, h_scr, slots)
    _stash(x_ref[0] + jnp.concatenate(ys, axis=1), tail, x1_scr, h_scr)


def _sgu_kernel(x_ref, gmix_ref, win_ref, bin_ref, lng_ref, lnb_ref, ws_ref, bsb_ref, wout_ref, bout_ref,
                *rest, tm, final):
    tail, (o_ref, x1_scr, h_scr) = _split_tail_refs(rest, final)
    vals = {}
    rows = []

    def gate_and_norm(pin):
        h = _rms(x_ref[0], gmix_ref[...]).astype(BF)
        z = jax.nn.gelu(_dot(h, win_ref[...]) + bin_ref[...])
        vals["u"] = z[:, :D]
        vals["v"] = _layer_norm(z[:, D:], lng_ref[...], lnb_ref[...]).astype(BF)

    def mix_chunk(c, pin):
        heads = []
        for hh in range(SGU_HEADS):
            blk = vals["v"][c * SGU_CHUNK:(c + 1) * SGU_CHUNK, hh * SGU_HEAD_DIM:(hh + 1) * SGU_HEAD_DIM]
            heads.append(_dot(ws_ref[hh], blk) + bsb_ref[hh])
        rows.append(jnp.concatenate(heads, axis=1))

    def project(pin):
        vm = jnp.concatenate(rows, axis=0)
        vals["y"] = _dot((vals["u"] * vm).astype(BF), wout_ref[...]) + bout_ref[...]

    pieces = [gate_and_norm] + [functools.partial(mix_chunk, c) for c in range(tm // SGU_CHUNK)] + [project]
    slots = [a + b for a, b in zip(_spread(pieces[:1], 0, 0), _spread(pieces[1:], MLP_SLOTS // 2, MLP_SLOTS - 1))]
    _skewed_mlp(pl.program_id(0), tail, o_ref, x1_scr, h_scr, slots)
    _stash(x_ref[0] + vals["y"], tail, x1_scr, h_scr)


def _fourier1_kernel(x_ref, gmix_ref, g1_ref, yr_ref, yi_ref, *, bd):
    g = gmix_ref[...]
    xt = pltpu.einshape("bjd->jbd", x_ref[0])
    for j in range(SUBLANES):
        h = _rms(xt[j], g).astype(BF)
        y = _dot(g1_ref[j], h)
        yr_ref[0, j] = y[:bd].astype(BF)
        yi_ref[0, j] = y[bd:].astype(BF)


def _fourier2_kernel(x_ref, yr_ref, yi_ref, p_ref, cc_ref, sc_ref, wout_ref, bout_ref, *rest, nb, ar, final):
    (gmlp_ref, wup_ref, wdn_ref, gfin_ref), (o_ref,) = _split_tail_refs(rest, final)
    qr, qi = [], []
    for bb in range(nb):
        ycat = jnp.concatenate([yr_ref[bb].reshape(ar, D), yi_ref[bb].reshape(ar, D)], axis=0)
        q = _dot(p_ref[...], ycat)
        qr.append(q[:ar])
        qi.append(q[ar:])
    qr = jnp.concatenate(qr, axis=0).astype(BF)
    qi = jnp.concatenate(qi, axis=0).astype(BF)
    fs = []
    for gi in range(D // FNET_GROUP):
        cols = slice(gi * FNET_GROUP, (gi + 1) * FNET_GROUP)
        fs.append(_dot(qr[:, cols], cc_ref[...]) + _dot(qi[:, cols], sc_ref[...]))
    f = jnp.concatenate(fs, axis=1)
    y = _dot(f.astype(BF), wout_ref[...]) + bout_ref[...]
    x1 = x_ref[...].reshape(nb * ar, D) + y
    h = _rms(x1, gmlp_ref[...]).astype(BF)
    out = _mlp(lambda: h, lambda: x1, wup_ref, wdn_ref, gfin_ref, ff_chunk=WIDE_FF_CHUNK)
    o_ref[...] = out.reshape(o_ref.shape)


def _resident(shape):
    return pl.BlockSpec(shape, lambda *_: (0,) * len(shape), pipeline_mode=pl.Buffered(1))


def _params(n_axes):
    return pltpu.CompilerParams(dimension_semantics=("arbitrary",) * n_axes,
                                vmem_limit_bytes=V7X_VMEM_LIMIT)


def _row(v):
    return v.reshape(1, -1).astype(F32)


def _layer_block(shape, layer):
    return pl.BlockSpec((None,) + shape, lambda *_: (layer,) + (0,) * len(shape),
                        pipeline_mode=pl.Buffered(1))


def _tail_operands(gmlp, wup, wdn, layer, gfin):
    ops = [_row(gmlp), wup, wdn]
    specs = [_resident((1, D)), _layer_block((D, D_FF), layer), _layer_block((D_FF, D), layer)]
    if gfin is not None:
        ops.append(_row(gfin))
        specs.append(_resident((1, D)))
    return ops, specs


def _skewed_specs(tm, seq, n_tiles, halo):
    per_seq = seq // tm

    def tile(s):
        st = jnp.minimum(s, n_tiles - 1)
        return st // per_seq, st % per_seq

    def main(s):
        b, i = tile(s)
        return b, i, 0

    def out(s):
        so = jnp.maximum(s - 1, 0)
        return so // per_seq, so % per_seq, 0

    specs = [pl.BlockSpec((1, tm, D), main)]
    if halo:
        per = tm // halo
        nh = seq // halo

        def prev(s):
            b, i = tile(s)
            return b, jnp.maximum(i * per - 1, 0), 0

        def nxt(s):
            b, i = tile(s)
            return b, jnp.minimum((i + 1) * per, nh - 1), 0

        specs += [pl.BlockSpec((1, halo, D), prev), pl.BlockSpec((1, halo, D), nxt)]
    return specs, pl.BlockSpec((1, tm, D), out)


def _stash_scratch(tm):
    return [pltpu.VMEM((tm, D), F32), pltpu.VMEM((tm, D), BF)]


def _pool_layer(x, gmix, pool_w, pool_scale, gmlp, wup, wdn, layer, gfin):
    bsz, seq, _ = x.shape
    tm = min(ROW_TILE, seq)
    n_tiles = bsz * (seq // tm)
    tail_ops, tail_specs = _tail_operands(gmlp, wup, wdn, layer, gfin)
    x_specs, out_spec = _skewed_specs(tm, seq, n_tiles, POOL_HALO)
    kern = functools.partial(_pool_kernel, tm=tm, seq=seq, n_tiles=n_tiles, final=gfin is not None)
    return pl.pallas_call(
        kern,
        grid=(n_tiles + 1,),
        in_specs=x_specs + [_resident((1, D)), _resident(pool_w.shape), _resident((1, D))] + tail_specs,
        out_specs=out_spec,
        out_shape=jax.ShapeDtypeStruct(x.shape, F32),
        scratch_shapes=[pltpu.VMEM((tm + 2 * POOL_HALO, D), F32)] + _stash_scratch(tm),
        compiler_params=_params(1),
        name="pool_mlp",
    )(x, x, x, _row(gmix), pool_w.astype(BF), _row(pool_scale), *tail_ops)


def _conv_layer(x, gmix, w_in, b_in, dw, dw_b, ln_g, ln_b, w_out, b_out, gmlp, wup, wdn, layer, gfin):
    bsz, seq, _ = x.shape
    tm = min(ROW_TILE, seq)
    n_tiles = bsz * (seq // tm)
    tail_ops, tail_specs = _tail_operands(gmlp, wup, wdn, layer, gfin)
    x_specs, out_spec = _skewed_specs(tm, seq, n_tiles, CONV_HALO)
    dw_pad = jnp.zeros((CONV_WIDTH + 1, D), F32).at[:CONV_WIDTH].set(dw)
    kern = functools.partial(_conv_kernel, tm=tm, seq=seq, n_tiles=n_tiles, final=gfin is not None)
    ext = tm + 2 * CONV_HALO
    return pl.pallas_call(
        kern,
        grid=(n_tiles + 1,),
        in_specs=x_specs + [
            _resident((1, D)), _resident((D, 2 * D)), _resident((1, 2 * D)),
            _resident((CONV_WIDTH + 1, D)), _resident((1, D)), _resident((1, D)), _resident((1, D)),
            _resident((D, D)), _resident((1, D))] + tail_specs,
        out_specs=out_spec,
        out_shape=jax.ShapeDtypeStruct(x.shape, F32),
        scratch_shapes=[pltpu.VMEM((ext, D), BF), pltpu.VMEM((ext, D), F32),
                        pltpu.VMEM((2, tm + SUBLANES, D), F32)] + _stash_scratch(tm),
        compiler_params=_params(1),
        name="conv_mlp",
    )(x, x, x, _row(gmix), w_in.astype(BF), _row(b_in), dw_pad, _row(dw_b), _row(ln_g), _row(ln_b),
      w_out.astype(BF), _row(b_out), *tail_ops)


def _sgu_layer(x, gmix, w_in, b_in, ln_g, ln_b, ws, bs, w_out, b_out, gmlp, wup, wdn, layer, gfin):
    bsz, seq, _ = x.shape
    tm = min(ROW_TILE, seq)
    n_tiles = bsz * (seq // tm)
    tail_ops, tail_specs = _tail_operands(gmlp, wup, wdn, layer, gfin)
    x_specs, out_spec = _skewed_specs(tm, seq, n_tiles, 0)
    bs_b = jnp.broadcast_to(bs.astype(F32)[:, :, None], (SGU_HEADS, SGU_CHUNK, SGU_HEAD_DIM))
    kern = functools.partial(_sgu_kernel, tm=tm, final=gfin is not None)
    return pl.pallas_call(
        kern,
        grid=(n_tiles + 1,),
        in_specs=x_specs + [
            _resident((1, D)), _resident((D, 2 * D)), _resident((1, 2 * D)),
            _resident((1, D)), _resident((1, D)),
            _resident((SGU_HEADS, SGU_CHUNK, SGU_CHUNK)),
            _resident((SGU_HEADS, SGU_CHUNK, SGU_HEAD_DIM)),
            _resident((D, D)), _resident((1, D))] + tail_specs,
        out_specs=out_spec,
        out_shape=jax.ShapeDtypeStruct(x.shape, F32),
        scratch_shapes=_stash_scratch(tm),
        compiler_params=_params(1),
        name="sgu_mlp",
    )(x, _row(gmix), w_in.astype(BF), _row(b_in), _row(ln_g), _row(ln_b), ws.astype(BF), bs_b,
      w_out.astype(BF), _row(b_out), *tail_ops)


def _dft_tables(seq):
    bd = FNET_BD
    a_len = seq // bd
    r = FNET_R

    def cs(n, rows, cols):
        ang = 2.0 * np.pi * ((np.outer(rows, cols)) % n) / n
        return np.cos(ang), np.sin(ang)

    n = a_len * np.arange(bd)[None, None, :] + np.arange(a_len)[:, None, None]
    ang = 2.0 * np.pi * ((np.arange(bd)[None, :, None] * n) % seq) / seq
    g1 = np.concatenate([np.cos(ang), -np.sin(ang)], axis=1) / np.sqrt(bd)
    c2, s2 = cs(a_len, np.arange(a_len), np.arange(a_len))
    eye = np.eye(r)
    kc, ks = np.kron(c2, eye), np.kron(s2, eye)
    p = np.block([[kc, ks], [-ks, kc]]) / np.sqrt(a_len)
    cc, sc = cs(FNET_GROUP, np.arange(FNET_GROUP), np.arange(FNET_GROUP))
    cc, sc = cc / np.sqrt(FNET_GROUP), sc / np.sqrt(FNET_GROUP)
    as_bf = lambda m: jnp.asarray(m, dtype=F32).astype(BF)
    return as_bf(g1), as_bf(p), as_bf(cc), as_bf(sc)


def _fourier_layer(x, gmix, w_out, b_out, gmlp, wup, wdn, layer, gfin):
    bsz, seq, _ = x.shape
    bd, r = FNET_BD, FNET_R
    a_len = seq // bd
    ar = a_len * r
    nb = max(1, min(bsz, ROW_TILE // ar))
    g1, p, cc, sc = _dft_tables(seq)
    yshape = jax.ShapeDtypeStruct((bsz, a_len, bd, D), BF)
    yr, yi = pl.pallas_call(
        functools.partial(_fourier1_kernel, bd=bd),
        grid=(a_len // SUBLANES, bsz),
        in_specs=[pl.BlockSpec((1, bd, SUBLANES, D), lambda j, b: (b, 0, j, 0)),
                  _resident((1, D)),
                  pl.BlockSpec((SUBLANES, 2 * bd, bd), lambda j, b: (j, 0, 0))],
        out_specs=[pl.BlockSpec((1, SUBLANES, bd, D), lambda j, b: (b, j, 0, 0)),
                   pl.BlockSpec((1, SUBLANES, bd, D), lambda j, b: (b, j, 0, 0))],
        out_shape=[yshape, yshape],
        compiler_params=_params(2),
        name="fourier_stage1",
    )(x.reshape(bsz, bd, a_len, D), _row(gmix), g1)

    tail_ops, tail_specs = _tail_operands(gmlp, wup, wdn, layer, gfin)
    blk = pl.BlockSpec((nb, a_len, r, D), lambda b, k: (b, 0, k, 0))
    out = pl.pallas_call(
        functools.partial(_fourier2_kernel, nb=nb, ar=ar, final=gfin is not None),
        grid=(bsz // nb, bd // r),
        in_specs=[blk, blk, blk, _resident((2 * ar, 2 * ar)),
                  _resident((FNET_GROUP, FNET_GROUP)), _resident((FNET_GROUP, FNET_GROUP)),
                  _resident((D, D)), _resident((1, D))] + tail_specs,
        out_specs=blk,
        out_shape=jax.ShapeDtypeStruct((bsz, a_len, bd, D), F32),
        compiler_params=_params(2),
        name="fourier_stage2_mlp",
    )(x.reshape(bsz, a_len, bd, D), yr, yi, p, cc, sc, w_out.astype(BF), _row(b_out), *tail_ops)
    return out.reshape(bsz, seq, D)


def _trunk(x, norm_mix_g, norm_mlp_g, w_up, w_down, final_norm_g, mixers):
    depth = norm_mix_g.shape[0]
    for i in range(depth):
        kind, j = i % 4, i // 4
        gfin = final_norm_g if i == depth - 1 else None
        common = (norm_mlp_g[i], w_up, w_down, i, gfin)
        x = mixers[kind](x, norm_mix_g[i], j, *common)
    return x


def kernel(x_prompt, x_sample, norm_mix_g, norm_mlp_g, w_up, w_down, final_norm_g, pool_w, pool_scale,
           conv_w_in, conv_b_in, conv_dw, conv_dw_b, conv_ln_g, conv_ln_b, conv_w_out, conv_b_out,
           sgu_w_in, sgu_b_in, sgu_ln_g, sgu_ln_b, sgu_ws, sgu_bs, sgu_w_out, sgu_b_out,
           fnet_w_out, fnet_b_out):
    w_up = w_up.astype(BF)
    w_down = w_down.astype(BF)

    def pool(x, g, j, *common):
        return _pool_layer(x, g, pool_w[j], pool_scale[j], *common)

    def conv(x, g, j, *common):
        return _conv_layer(x, g, conv_w_in[j], conv_b_in[j], conv_dw[j], conv_dw_b[j], conv_ln_g[j],
                           conv_ln_b[j], conv_w_out[j], conv_b_out[j], *common)

    def sgu(x, g, j, *common):
        return _sgu_layer(x, g, sgu_w_in[j], sgu_b_in[j], sgu_ln_g[j], sgu_ln_b[j], sgu_ws[j], sgu_bs[j],
                          sgu_w_out[j], sgu_b_out[j], *common)

    def fourier(x, g, j, *common):
        return _fourier_layer(x, g, fnet_w_out[j], fnet_b_out[j], *common)

    mixers = (pool, conv, sgu, fourier)
    y_prompt = _trunk(x_prompt, norm_mix_g, norm_mlp_g, w_up, w_down, final_norm_g, mixers)
    y_sample = _trunk(x_sample, norm_mix_g, norm_mlp_g, w_up, w_down, final_norm_g, mixers)
    return (y_prompt, y_sample)
```

```python
import functools

import numpy as np
import jax
import jax.numpy as jnp
from jax import lax
from jax.experimental import pallas as pl
from jax.experimental.pallas import tpu as pltpu

D = 1024
D_FF = 4 * D
EPS = 1e-6
POOL_WINDOWS = (2, 4, 8, 16)
POOL_GROUP = D // len(POOL_WINDOWS)
POOL_HALO = 8
CONV_WIDTH = 31
CONV_HALO = 16
SGU_HEADS = 4
SGU_HEAD_DIM = D // SGU_HEADS
SGU_CHUNK = 128
FNET_GROUP = 256
FNET_BD = 256
FNET_R = 16
SUBLANES = 8
MXU_COLS = 256

FF_CHUNK = 1024
WIDE_FF_CHUNK = 2048
ROW_TILE = 512
V7X_VMEM_LIMIT = 56 * 1024 * 1024


def _mlp_slots(ff_chunk):
    return (D_FF // ff_chunk) * (ff_chunk // MXU_COLS + D // MXU_COLS)


MLP_SLOTS = _mlp_slots(FF_CHUNK)

BF = jnp.bfloat16
F32 = jnp.float32


def _rms(x, g):
    return x * lax.rsqrt(jnp.mean(x * x, axis=-1, keepdims=True) + EPS) * g


def _layer_norm(x, g, b):
    mu = jnp.mean(x, axis=-1, keepdims=True)
    xc = x - mu
    var = jnp.mean(xc * xc, axis=-1, keepdims=True)
    return xc * lax.rsqrt(var + EPS) * g + b


def _dot(a, b):
    return jnp.dot(a, b, preferred_element_type=F32)


def _mlp(load_h, load_x1, wup_ref, wdn_ref, gfin_ref, side_work=(), ff_chunk=FF_CHUNK):
    per_chunk = ff_chunk // MXU_COLS
    n_out = D // MXU_COLS
    slot = [0]

    def run_side(result):
        pieces = side_work[slot[0]] if side_work else ()
        slot[0] += 1
        if pieces:
            pin = jnp.concatenate([jnp.minimum(jnp.abs(result[-1:, -128:]), 0.0)] * (D // 128), axis=1)
            for piece in pieces:
                piece(pin)

    acc = [None] * n_out
    for c in range(D_FF // ff_chunk):
        parts = []
        for n in range(per_chunk):
            lo = c * ff_chunk + n * MXU_COLS
            a = _dot(load_h(), wup_ref[:, lo:lo + MXU_COLS])
            ab = jnp.maximum(a.astype(BF), 0.0)
            parts.append(ab * ab)
            run_side(a)
        a = jnp.concatenate(parts, axis=1)
        for n in range(n_out):
            d = _dot(a, wdn_ref[c * ff_chunk:(c + 1) * ff_chunk, n * MXU_COLS:(n + 1) * MXU_COLS])
            acc[n] = d if acc[n] is None else acc[n] + d
            run_side(d)
    out = load_x1() + jnp.concatenate(acc, axis=1)
    if gfin_ref is not None:
        out = _rms(out, gfin_ref[...])
    return out


def _spread(pieces, first, last, n_slots=MLP_SLOTS):
    slots = [[] for _ in range(n_slots)]
    for idx, piece in enumerate(pieces):
        slots[first + idx * (last - first + 1) // len(pieces)].append(piece)
    return slots


def _split_tail_refs(refs, final):
    n = 4 if final else 3
    return tuple(refs[:n]) + (() if final else (None,)), refs[n:]


def _skewed_mlp(step, tail, o_ref, x1_scr, h_scr, side_work=(), ff_chunk=FF_CHUNK):
    @pl.when(step == 0)
    def _():
        x1_scr[...] = jnp.zeros_like(x1_scr)
        h_scr[...] = jnp.zeros_like(h_scr)

    _, wup_ref, wdn_ref, gfin_ref = tail
    o_ref[0] = _mlp(lambda: h_scr[...], lambda: x1_scr[...], wup_ref, wdn_ref, gfin_ref, side_work, ff_chunk)


def _stash(x1, tail, x1_scr, h_scr):
    x1_scr[...] = x1
    h_scr[...] = _rms(x1, tail[0][...]).astype(BF)


def _seq_tile(step, n_tiles, tiles_per_seq):
    return jnp.minimum(step, n_tiles - 1) % tiles_per_seq


def _pool_kernel(x_ref, xp_ref, xn_ref, gmix_ref, pw_ref, psc_ref, *rest, tm, seq, n_tiles, final):
    tail, (o_ref, hext_ref, x1_scr, h_scr) = _split_tail_refs(rest, final)
    step = pl.program_id(0)
    i = _seq_tile(step, n_tiles, seq // tm)
    last = seq // tm - 1
    ys = []

    def norm_rows(pin):
        g = gmix_ref[...]
        hext_ref[0:POOL_HALO, :] = jnp.where(i > 0, _rms(xp_ref[0], g), 0.0)
        hext_ref[POOL_HALO:POOL_HALO + tm, :] = _rms(x_ref[0], g)
        hext_ref[POOL_HALO + tm:2 * POOL_HALO + tm, :] = jnp.where(i < last, _rms(xn_ref[0], g), 0.0)

    def group(gi, win, pin):
        half = win // 2
        cols = slice(gi * POOL_GROUP, (gi + 1) * POOL_GROUP)
        t = i * tm + lax.broadcasted_iota(jnp.int32, (tm, 1), 0)
        s = hext_ref[POOL_HALO - half:POOL_HALO - half + tm, cols]
        for j in range(1, win):
            s = s + hext_ref[POOL_HALO - half + j:POOL_HALO - half + j + tm, cols]
        cnt = (jnp.minimum(t + half, seq) - jnp.maximum(t - half, 0)).astype(F32)
        d = s / cnt - hext_ref[POOL_HALO:POOL_HALO + tm, cols]
        ys.append(_dot(d.astype(BF), pw_ref[gi]))

    pieces = [norm_rows] + [functools.partial(group, gi, win) for gi, win in enumerate(POOL_WINDOWS)]
    n_slots = _mlp_slots(WIDE_FF_CHUNK)
    _skewed_mlp(step, tail, o_ref, x1_scr, h_scr, _spread(pieces, 0, n_slots - 1, n_slots), WIDE_FF_CHUNK)
    y = jnp.concatenate(ys, axis=1) * psc_ref[...]
    _stash(x_ref[0] + y, tail, x1_scr, h_scr)


def _conv_kernel(x_ref, xp_ref, xn_ref, gmix_ref, win_ref, bin_ref, dw_ref, dwb_ref, lng_ref, lnb_ref,
                 wout_ref, bout_ref, *rest, tm, seq, n_tiles, final):
    tail, (o_ref, hext_ref, gext_ref, part_ref, x1_scr, h_scr) = _split_tail_refs(rest, final)
    step = pl.program_id(0)
    i = _seq_tile(step, n_tiles, seq // tm)
    last = seq // tm - 1
    ext = tm + 2 * CONV_HALO
    base = CONV_HALO - CONV_WIDTH // 2
    n_blk = D // MXU_COLS
    acc = [None] * n_blk
    ys = []

    def norm_rows(pin):
        g = gmix_ref[...]
        hext_ref[0:CONV_HALO, :] = _rms(xp_ref[0], g).astype(BF)
        hext_ref[CONV_HALO:CONV_HALO + tm, :] = _rms(x_ref[0], g).astype(BF)
        hext_ref[CONV_HALO + tm:ext, :] = _rms(xn_ref[0], g).astype(BF)

    def gated_cols(cb, pin):
        cols = slice(cb * MXU_COLS, (cb + 1) * MXU_COLS)
        gcols = slice(D + cb * MXU_COLS, D + (cb + 1) * MXU_COLS)
        val = _dot(hext_ref[...], win_ref[:, cols]) + bin_ref[:, cols]
        gate = _dot(hext_ref[...], win_ref[:, gcols]) + bin_ref[:, gcols]
        r = lax.broadcasted_iota(jnp.int32, (ext, 1), 0)
        inside = jnp.logical_and(jnp.logical_or(r >= CONV_HALO, i > 0),
                                 jnp.logical_or(r < CONV_HALO + tm, i < last))
        gext_ref[:, cols] = jnp.where(inside, val * jax.nn.sigmoid(gate), 0.0)

    def taps(cb, mis, pin):
        cols = slice(cb * MXU_COLS, (cb + 1) * MXU_COLS)
        part = None
        for q in range((base + CONV_WIDTH - 1) // SUBLANES + 1):
            k = SUBLANES * q + mis - base
            if 0 <= k < CONV_WIDTH:
                term = (gext_ref[SUBLANES * q:SUBLANES * q + tm + SUBLANES, cols]
                        * (dw_ref[k:k + 1, cols] + pin[:, cols]))
                part = term if part is None else part + term
        if mis == 0:
            shifted = part[0:tm]
        else:
            part_ref[mis % 2, :, cols] = part
            shifted = part_ref[mis % 2, mis:mis + tm, cols]
        acc[cb] = shifted if acc[cb] is None else acc[cb] + shifted

    def activate(pin):
        c = jnp.concatenate(acc, axis=1) + dwb_ref[...]
        hext_ref[0:tm, :] = jax.nn.silu(_layer_norm(c, lng_ref[...], lnb_ref[...])).astype(BF)

    def project(cb, pin):
        cols = slice(cb * MXU_COLS, (cb + 1) * MXU_COLS)
        ys.append(_dot(hext_ref[0:tm, :], wout_ref[:, cols]) + bout_ref[:, cols])

    head = [norm_rows] + [functools.partial(gated_cols, cb) for cb in range(n_blk)]
    body = [functools.partial(taps, cb, mis) for cb in range(n_blk) for mis in range(SUBLANES)]
    foot = [activate] + [functools.partial(project, cb) for cb in range(n_blk)]
    slots = [a + b + c for a, b, c in zip(_spread(head, 0, 1), _spread(body, 2, MLP_SLOTS - 6),
                                          _spread(foot, MLP_SLOTS - 5, MLP_SLOTS - 1))]
    _skewed_mlp(step, tail, o_ref, x1_scr, h_scr, slots)
    _stash(x_ref[0] + jnp.concatenate(ys, axis=1), tail, x1_scr, h_scr)


def _sgu_kernel(x_ref, gmix_ref, win_ref, bin_ref, lng_ref, lnb_ref, ws_ref, bsb_ref, wout_ref, bout_ref,
                *rest, tm, final):
    tail, (o_ref, x1_scr, h_scr) = _split_tail_refs(rest, final)
    vals = {}
    rows = []

    def gate_and_norm(pin):
        h = _rms(x_ref[0], gmix_ref[...]).astype(BF)
        z = jax.nn.gelu(_dot(h, win_ref[...]) + bin_ref[...])
        vals["u"] = z[:, :D]
        vals["v"] = _layer_norm(z[:, D:], lng_ref[...], lnb_ref[...]).astype(BF)

    def mix_chunk(c, pin):
        heads = []
        for hh in range(SGU_HEADS):
            blk = vals["v"][c * SGU_CHUNK:(c + 1) * SGU_CHUNK, hh * SGU_HEAD_DIM:(hh + 1) * SGU_HEAD_DIM]
            heads.append(_dot(ws_ref[hh], blk) + bsb_ref[hh])
        rows.append(jnp.concatenate(heads, axis=1))

    def project(pin):
        vm = jnp.concatenate(rows, axis=0)
        vals["y"] = _dot((vals["u"] * vm).astype(BF), wout_ref[...]) + bout_ref[...]

    pieces = [gate_and_norm] + [functools.partial(mix_chunk, c) for c in range(tm // SGU_CHUNK)] + [project]
    _skewed_mlp(pl.program_id(0), tail, o_ref, x1_scr, h_scr, _spread(pieces, 0, MLP_SLOTS - 1))
    _stash(x_ref[0] + vals["y"], tail, x1_scr, h_scr)


def _fourier1_kernel(x_ref, gmix_ref, g1_ref, yr_ref, yi_ref, *, bd):
    g = gmix_ref[...]
    xt = pltpu.einshape("bjd->jbd", x_ref[0])
    for j in range(SUBLANES):
        h = _rms(xt[j], g).astype(BF)
        y = _dot(g1_ref[j], h)
        yr_ref[0, j] = y[:bd].astype(BF)
        yi_ref[0, j] = y[bd:].astype(BF)


def _fourier2_kernel(x_ref, yr_ref, yi_ref, p_ref, cc_ref, sc_ref, wout_ref, bout_ref, *rest, nb, ar, final):
    tail, (o_ref, x1_scr, h_scr) = _split_tail_refs(rest, final)
    _, wup_ref, wdn_ref, gfin_ref = tail
    qr, qi, fs, vals = [], [], [], {}

    @pl.when(pl.program_id(0) == 0)
    def _():
        x1_scr[...] = jnp.zeros_like(x1_scr)
        h_scr[...] = jnp.zeros_like(h_scr)

    def stage2(bb, pin):
        ycat = jnp.concatenate([yr_ref[bb].reshape(ar, D), yi_ref[bb].reshape(ar, D)], axis=0)
        q = _dot(p_ref[...], ycat)
        qr.append(q[:ar].astype(BF))
        qi.append(q[ar:].astype(BF))

    def channels(gi, pin):
        cols = slice(gi * FNET_GROUP, (gi + 1) * FNET_GROUP)
        r_, i_ = jnp.concatenate(qr, axis=0), jnp.concatenate(qi, axis=0)
        fs.append(_dot(r_[:, cols], cc_ref[...]) + _dot(i_[:, cols], sc_ref[...]))

    def project(pin):
        f = jnp.concatenate(fs, axis=1)
        vals["y"] = _dot(f.astype(BF), wout_ref[...]) + bout_ref[...]

    pieces = ([functools.partial(stage2, bb) for bb in range(nb)]
              + [functools.partial(channels, gi) for gi in range(D // FNET_GROUP)] + [project])
    n_slots = _mlp_slots(WIDE_FF_CHUNK)
    out = _mlp(lambda: h_scr[...], lambda: x1_scr[...], wup_ref, wdn_ref, gfin_ref,
               _spread(pieces, 0, n_slots - 1, n_slots), WIDE_FF_CHUNK)
    o_ref[...] = out.reshape(o_ref.shape)
    _stash(x_ref[...].reshape(nb * ar, D) + vals["y"], tail, x1_scr, h_scr)


def _resident(shape):
    return pl.BlockSpec(shape, lambda *_: (0,) * len(shape), pipeline_mode=pl.Buffered(1))


def _params(n_axes):
    return pltpu.CompilerParams(dimension_semantics=("arbitrary",) * n_axes,
                                vmem_limit_bytes=V7X_VMEM_LIMIT)


def _row(v):
    return v.reshape(1, -1).astype(F32)


def _layer_block(shape, layer):
    return pl.BlockSpec((None,) + shape, lambda *_: (layer,) + (0,) * len(shape),
                        pipeline_mode=pl.Buffered(1))


def _tail_operands(gmlp, wup, wdn, layer, gfin):
    ops = [_row(gmlp), wup, wdn]
    specs = [_resident((1, D)), _layer_block((D, D_FF), layer), _layer_block((D_FF, D), layer)]
    if gfin is not None:
        ops.append(_row(gfin))
        specs.append(_resident((1, D)))
    return ops, specs


def _skewed_specs(tm, seq, n_tiles, halo):
    per_seq = seq // tm

    def tile(s):
        st = jnp.minimum(s, n_tiles - 1)
        return st // per_seq, st % per_seq

    def main(s):
        b, i = tile(s)
        return b, i, 0

    def out(s):
        so = jnp.maximum(s - 1, 0)
        return so // per_seq, so % per_seq, 0

    specs = [pl.BlockSpec((1, tm, D), main)]
    if halo:
        per = tm // halo
        nh = seq // halo

        def prev(s):
            b, i = tile(s)
            return b, jnp.maximum(i * per - 1, 0), 0

        def nxt(s):
            b, i = tile(s)
            return b, jnp.minimum((i + 1) * per, nh - 1), 0

        specs += [pl.BlockSpec((1, halo, D), prev), pl.BlockSpec((1, halo, D), nxt)]
    return specs, pl.BlockSpec((1, tm, D), out)


def _stash_scratch(tm):
    return [pltpu.VMEM((tm, D), F32), pltpu.VMEM((tm, D), BF)]


def _pool_layer(x, gmix, pool_w, pool_scale, gmlp, wup, wdn, layer, gfin):
    bsz, seq, _ = x.shape
    tm = min(ROW_TILE, seq)
    n_tiles = bsz * (seq // tm)
    tail_ops, tail_specs = _tail_operands(gmlp, wup, wdn, layer, gfin)
    x_specs, out_spec = _skewed_specs(tm, seq, n_tiles, POOL_HALO)
    kern = functools.partial(_pool_kernel, tm=tm, seq=seq, n_tiles=n_tiles, final=gfin is not None)
    return pl.pallas_call(
        kern,
        grid=(n_tiles + 1,),
        in_specs=x_specs + [_resident((1, D)), _resident(pool_w.shape), _resident((1, D))] + tail_specs,
        out_specs=out_spec,
        out_shape=jax.ShapeDtypeStruct(x.shape, F32),
        scratch_shapes=[pltpu.VMEM((tm + 2 * POOL_HALO, D), F32)] + _stash_scratch(tm),
        compiler_params=_params(1),
        name="pool_mlp",
    )(x, x, x, _row(gmix), pool_w.astype(BF), _row(pool_scale), *tail_ops)


def _conv_layer(x, gmix, w_in, b_in, dw, dw_b, ln_g, ln_b, w_out, b_out, gmlp, wup, wdn, layer, gfin):
    bsz, seq, _ = x.shape
    tm = min(ROW_TILE, seq)
    n_tiles = bsz * (seq // tm)
    tail_ops, tail_specs = _tail_operands(gmlp, wup, wdn, layer, gfin)
    x_specs, out_spec = _skewed_specs(tm, seq, n_tiles, CONV_HALO)
    dw_pad = jnp.zeros((CONV_WIDTH + 1, D), F32).at[:CONV_WIDTH].set(dw)
    kern = functools.partial(_conv_kernel, tm=tm, seq=seq, n_tiles=n_tiles, final=gfin is not None)
    ext = tm + 2 * CONV_HALO
    return pl.pallas_call(
        kern,
        grid=(n_tiles + 1,),
        in_specs=x_specs + [
            _resident((1, D)), _resident((D, 2 * D)), _resident((1, 2 * D)),
            _resident((CONV_WIDTH + 1, D)), _resident((1, D)), _resident((1, D)), _resident((1, D)),
            _resident((D, D)), _resident((1, D))] + tail_specs,
        out_specs=out_spec,
        out_shape=jax.ShapeDtypeStruct(x.shape, F32),
        scratch_shapes=[pltpu.VMEM((ext, D), BF), pltpu.VMEM((ext, D), F32),
                        pltpu.VMEM((2, tm + SUBLANES, D), F32)] + _stash_scratch(tm),
        compiler_params=_params(1),
        name="conv_mlp",
    )(x, x, x, _row(gmix), w_in.astype(BF), _row(b_in), dw_pad, _row(dw_b), _row(ln_g), _row(ln_b),
      w_out.astype(BF), _row(b_out), *tail_ops)


def _sgu_layer(x, gmix, w_in, b_in, ln_g, ln_b, ws, bs, w_out, b_out, gmlp, wup, wdn, layer, gfin):
    bsz, seq, _ = x.shape
    tm = min(ROW_TILE, seq)
    n_tiles = bsz * (seq // tm)
    tail_ops, tail_specs = _tail_operands(gmlp, wup, wdn, layer, gfin)
    x_specs, out_spec = _skewed_specs(tm, seq, n_tiles, 0)
    bs_b = jnp.broadcast_to(bs.astype(F32)[:, :, None], (SGU_HEADS, SGU_CHUNK, SGU_HEAD_DIM))
    kern = functools.partial(_sgu_kernel, tm=tm, final=gfin is not None)
    return pl.pallas_call(
        kern,
        grid=(n_tiles + 1,),
        in_specs=x_specs + [
            _resident((1, D)), _resident((D, 2 * D)), _resident((1, 2 * D)),
            _resident((1, D)), _resident((1, D)),
            _resident((SGU_HEADS, SGU_CHUNK, SGU_CHUNK)),
            _resident((SGU_HEADS, SGU_CHUNK, SGU_HEAD_DIM)),
            _resident((D, D)), _resident((1, D))] + tail_specs,
        out_specs=out_spec,
        out_shape=jax.ShapeDtypeStruct(x.shape, F32),
        scratch_shapes=_stash_scratch(tm),
        compiler_params=_params(1),
        name="sgu_mlp",
    )(x, _row(gmix), w_in.astype(BF), _row(b_in), _row(ln_g), _row(ln_b), ws.astype(BF), bs_b,
      w_out.astype(BF), _row(b_out), *tail_ops)


def _dft_tables(seq):
    bd = FNET_BD
    a_len = seq // bd
    r = FNET_R

    def cs(n, rows, cols):
        ang = 2.0 * np.pi * ((np.outer(rows, cols)) % n) / n
        return np.cos(ang), np.sin(ang)

    n = a_len * np.arange(bd)[None, None, :] + np.arange(a_len)[:, None, None]
    ang = 2.0 * np.pi * ((np.arange(bd)[None, :, None] * n) % seq) / seq
    g1 = np.concatenate([np.cos(ang), -np.sin(ang)], axis=1) / np.sqrt(bd)
    c2, s2 = cs(a_len, np.arange(a_len), np.arange(a_len))
    eye = np.eye(r)
    kc, ks = np.kron(c2, eye), np.kron(s2, eye)
    p = np.block([[kc, ks], [-ks, kc]]) / np.sqrt(a_len)
    cc, sc = cs(FNET_GROUP, np.arange(FNET_GROUP), np.arange(FNET_GROUP))
    cc, sc = cc / np.sqrt(FNET_GROUP), sc / np.sqrt(FNET_GROUP)
    as_bf = lambda m: jnp.asarray(m, dtype=F32).astype(BF)
    return as_bf(g1), as_bf(p), as_bf(cc), as_bf(sc)


def _fourier_layer(x, gmix, w_out, b_out, gmlp, wup, wdn, layer, gfin):
    bsz, seq, _ = x.shape
    bd, r = FNET_BD, FNET_R
    a_len = seq // bd
    ar = a_len * r
    nb = max(1, min(bsz, ROW_TILE // ar))
    g1, p, cc, sc = _dft_tables(seq)
    yshape = jax.ShapeDtypeStruct((bsz, a_len, bd, D), BF)
    yr, yi = pl.pallas_call(
        functools.partial(_fourier1_kernel, bd=bd),
        grid=(a_len // SUBLANES, bsz),
        in_specs=[pl.BlockSpec((1, bd, SUBLANES, D), lambda j, b: (b, 0, j, 0)),
                  _resident((1, D)),
                  pl.BlockSpec((SUBLANES, 2 * bd, bd), lambda j, b: (j, 0, 0))],
        out_specs=[pl.BlockSpec((1, SUBLANES, bd, D), lambda j, b: (b, j, 0, 0)),
                   pl.BlockSpec((1, SUBLANES, bd, D), lambda j, b: (b, j, 0, 0))],
        out_shape=[yshape, yshape],
        compiler_params=_params(2),
        name="fourier_stage1",
    )(x.reshape(bsz, bd, a_len, D), _row(gmix), g1)

    tail_ops, tail_specs = _tail_operands(gmlp, wup, wdn, layer, gfin)
    per_b = bd // r
    n_blocks = (bsz // nb) * per_b

    def blk_in(s):
        st = jnp.minimum(s, n_blocks - 1)
        return st // per_b, 0, st % per_b, 0

    def blk_out(s):
        so = jnp.maximum(s - 1, 0)
        return so // per_b, 0, so % per_b, 0

    blk = pl.BlockSpec((nb, a_len, r, D), blk_in)
    out = pl.pallas_call(
        functools.partial(_fourier2_kernel, nb=nb, ar=ar, final=gfin is not None),
        grid=(n_blocks + 1,),
        in_specs=[blk, blk, blk, _resident((2 * ar, 2 * ar)),
                  _resident((FNET_GROUP, FNET_GROUP)), _resident((FNET_GROUP, FNET_GROUP)),
                  _resident((D, D)), _resident((1, D))] + tail_specs,
        out_specs=pl.BlockSpec((nb, a_len, r, D), blk_out),
        out_shape=jax.ShapeDtypeStruct((bsz, a_len, bd, D), F32),
        scratch_shapes=_stash_scratch(nb * ar),
        compiler_params=_params(1),
        name="fourier_stage2_mlp",
    )(x.reshape(bsz, a_len, bd, D), yr, yi, p, cc, sc, w_out.astype(BF), _row(b_out), *tail_ops)
    return out.reshape(bsz, seq, D)


def _trunk(x, norm_mix_g, norm_mlp_g, w_up, w_down, final_norm_g, mixers):
    depth = norm_mix_g.shape[0]
    for i in range(depth):
        kind, j = i % 4, i // 4
        gfin = final_norm_g if i == depth - 1 else None
        common = (norm_mlp_g[i], w_up, w_down, i, gfin)
        x = mixers[kind](x, norm_mix_g[i], j, *common)
    return x


def kernel(x_prompt, x_sample, norm_mix_g, norm_mlp_g, w_up, w_down, final_norm_g, pool_w, pool_scale,
           conv_w_in, conv_b_in, conv_dw, conv_dw_b, conv_ln_g, conv_ln_b, conv_w_out, conv_b_out,
           sgu_w_in, sgu_b_in, sgu_ln_g, sgu_ln_b, sgu_ws, sgu_bs, sgu_w_out, sgu_b_out,
           fnet_w_out, fnet_b_out):
    w_up = w_up.astype(BF)
    w_down = w_down.astype(BF)

    def pool(x, g, j, *common):
        return _pool_layer(x, g, pool_w[j], pool_scale[j], *common)

    def conv(x, g, j, *common):
        return _conv_layer(x, g, conv_w_in[j], conv_b_in[j], conv_dw[j], conv_dw_b[j], conv_ln_g[j],
                           conv_ln_b[j], conv_w_out[j], conv_b_out[j], *common)

    def sgu(x, g, j, *common):
        return _sgu_layer(x, g, sgu_w_in[j], sgu_b_in[j], sgu_ln_g[j], sgu_ln_b[j], sgu_ws[j], sgu_bs[j],
                          sgu_w_out[j], sgu_b_out[j], *common)

    def fourier(x, g, j, *common):
        return _fourier_layer(x, g, fnet_w_out[j], fnet_b_out[j], *common)

    mixers = (pool, conv, sgu, fourier)
    y_prompt = _trunk(x_prompt, norm_mix_g, norm_mlp_g, w_up, w_down, final_norm_g, mixers)
    y_sample = _trunk(x_sample, norm_mix_g, norm_mlp_g, w_up, w_down, final_norm_g, mixers)
    return (y_prompt, y_sample)
```
